```python
import jax, jax.numpy as jnp
from jax import lax
import numpy as np

D_MODEL = 1024
BATCH = 1
SEQ = 16384
DEPTH = 2

RWKV_HEADS = 8
RWKV_HEAD_DIM = 64
RWKV_WIDTH = RWKV_HEADS * RWKV_HEAD_DIM
DECAY_LORA = 64
AAA_LORA = 64
MV_LORA = 32
GATE_LORA = 128
GN_EPS = 64e-5
SG_GROUPS = 8
SG_GROUP_DIM = 64
SG_WIDTH = SG_GROUPS * SG_GROUP_DIM
SG_CHUNK = 128
RWKV_COLS = 3 * RWKV_WIDTH + DECAY_LORA + AAA_LORA + GATE_LORA
SG_COLS = 2 * SG_WIDTH
GATE_COLS = 2 * D_MODEL
IN_COLS = RWKV_COLS + SG_COLS + GATE_COLS
N_EXPERTS = 16
N_EXPERT_GROUPS = 4
EXPERTS_PER_GROUP = N_EXPERTS // N_EXPERT_GROUPS
TOP_K = 2
D_FF_EXPERT = 512
MOE_BLOCK = 128
LN_EPS = 1e-5
DEEPNORM_ALPHA = (2 * DEPTH) ** 0.25
DEEPNORM_BETA = (8 * DEPTH) ** -0.25

kernel_name = 'hybrid_rwkv7_spatialgate_grouped_moe_deepnorm'


def _ln(x, eps=LN_EPS):
    xf = x.astype(jnp.float32)
    mu = jnp.mean(xf, -1, keepdims=True)
    var = jnp.mean(jnp.square(xf - mu), -1, keepdims=True)
    return (xf - mu) * lax.rsqrt(var + eps)


def layer_norm(x, g, b, eps=LN_EPS):
    return (_ln(x, eps) * g + b).astype(x.dtype)


def token_shift(z, mu):
    prev = jnp.pad(z[:, :-1], ((0, 0), (1, 0), (0, 0)))
    return z + (prev - z) * mu


def wkv7_scan(r, w, k, v, a, b):
    B, T, H, N = r.shape
    seq = tuple(jnp.moveaxis(t.astype(jnp.float32), 1, 0) for t in (r, w, k, v, a, b))

    def step(S, inp):
        r_t, w_t, k_t, v_t, a_t, b_t = inp
        sa = jnp.einsum('bhvk,bhk->bhv', S, a_t)
        S = S * w_t[:, :, None, :] + sa[..., None] * b_t[:, :, None, :] + v_t[..., None] * k_t[:, :, None, :]
        return S, jnp.einsum('bhvk,bhk->bhv', S, r_t)

    S0 = jnp.zeros((B, H, N, N), jnp.float32)
    _, y = lax.scan(step, S0, seq)
    return jnp.moveaxis(y, 0, 1)


def rwkv7_branch(z, v_first, vmix, mu, w0, w2, a0, a2, g2, k_k, k_a, r_k, gn_g, gn_b):
    B, T, _ = z.shape
    H, N, W = RWKV_HEADS, RWKV_HEAD_DIM, RWKV_WIDTH
    z = token_shift(z, mu)
    o_w = 3 * W
    o_a = o_w + DECAY_LORA
    o_g = o_a + AAA_LORA
    r = z[..., :W]
    k = z[..., W:2 * W]
    v = z[..., 2 * W:3 * W]
    zw = z[..., o_w:o_a]
    za = z[..., o_a:o_g]
    zg = z[..., o_g:]
    w_log = -jax.nn.softplus(-(w0 + jnp.tanh(zw) @ w2)) - 0.5
    decay = jnp.exp(-jnp.exp(w_log.astype(jnp.float32)))
    a = jax.nn.sigmoid(a0 + za @ a2)
    g = jax.nn.sigmoid(zg) @ g2
    if vmix is None:
        v_first = v
    else:
        vm0, vm1, vm2 = vmix
        v = v + (v_first - v) * jax.nn.sigmoid(vm0 + (v @ vm1) @ vm2)
    heads = lambda t: t.reshape(B, T, H, N)
    kk = heads(k * k_k).astype(jnp.float32)
    kk = kk / jnp.maximum(jnp.sqrt(jnp.sum(kk * kk, -1, keepdims=True)), 1e-12)
    k = k * (1 + (a - 1) * k_a)
    rh, kh, vh, ah = heads(r), heads(k), heads(v), heads(a)
    y = wkv7_scan(rh, heads(decay), kh, vh, -kk, kk * ah.astype(jnp.float32))
    y = _ln(y, GN_EPS) * gn_g.reshape(H, N) + gn_b.reshape(H, N)
    y = y + jnp.sum(rh * kh * r_k, -1, keepdims=True) * vh
    return (y.reshape(B, T, W) * g).astype(z.dtype), v_first


def spatial_gating_branch(z, ln_g, ln_b, w_s, b_s):
    B, T, _ = z.shape
    z = jax.nn.gelu(z)
    u, v = z[..., :SG_WIDTH], z[..., SG_WIDTH:]
    v = layer_norm(v, ln_g, ln_b)
    v = v.reshape(B, T // SG_CHUNK, SG_CHUNK, SG_GROUPS, SG_GROUP_DIM)
    causal = jnp.tril(jnp.ones((SG_CHUNK, SG_CHUNK), bool))
    w_m = jnp.where(causal, w_s, 0.0)
    s = jnp.einsum('gts,bnsgd->bntgd', w_m, v) + b_s.T[:, :, None]
    return u * s.reshape(B, T, SG_WIDTH)


def group_limited_route(h, router_w, router_b):
    scores = jax.nn.sigmoid((h @ router_w).astype(jnp.float32))
    sel = (scores + router_b.astype(jnp.float32)).reshape(-1, N_EXPERT_GROUPS, EXPERTS_PER_GROUP)
    group_score = jnp.sum(lax.top_k(sel, TOP_K)[0], -1)
    best = jnp.argmax(group_score, -1)
    in_group = jnp.arange(N_EXPERT_GROUPS)[None, :] == best[:, None]
    masked = jnp.where(in_group[..., None], sel, -jnp.inf).reshape(-1, N_EXPERTS)
    _, idx = lax.top_k(masked, TOP_K)
    wts = jnp.take_along_axis(scores, idx, -1)
    wts = wts / jnp.sum(wts, -1, keepdims=True)
    return idx, wts


def moe_ffn(h, router_w, router_b, e_gate, e_up, e_down):
    B, T, D = h.shape
    hf = h.reshape(-1, D)
    N = hf.shape[0]
    idx, wts = group_limited_route(hf, router_w, router_b)
    A = N * TOP_K
    flat_e = idx.reshape(-1)
    flat_tok = jnp.repeat(jnp.arange(N, dtype=jnp.int32), TOP_K)
    flat_w = wts.reshape(-1)
    order = jnp.argsort(flat_e)
    e_sorted = flat_e[order]
    counts = jnp.bincount(flat_e, length=N_EXPERTS)
    padded = (counts + MOE_BLOCK - 1) // MOE_BLOCK * MOE_BLOCK
    pad_end = jnp.cumsum(padded)
    pad_start = pad_end - padded
    start = jnp.cumsum(counts) - counts
    dest = pad_start[e_sorted] + jnp.arange(A) - start[e_sorted]
    n_blocks = -(-A // MOE_BLOCK) + N_EXPERTS
    P = n_blocks * MOE_BLOCK
    slot_tok = jnp.zeros((P,), jnp.int32).at[dest].set(flat_tok[order])
    slot_w = jnp.zeros((P,), jnp.float32).at[dest].set(flat_w[order])
    block_expert = jnp.minimum(jnp.searchsorted(pad_end, jnp.arange(n_blocks) * MOE_BLOCK, side='right'), N_EXPERTS - 1)
    xb = hf[slot_tok].reshape(n_blocks, MOE_BLOCK, D)

    def expert_block(args):
        xblk, e = args
        hid = jax.nn.silu(xblk @ e_gate[e]) * (xblk @ e_up[e])
        return hid @ e_down[e]

    yb = lax.map(expert_block, (xb, block_expert)).reshape(P, D)
    y = jax.ops.segment_sum(yb * slot_w[:, None].astype(yb.dtype), slot_tok, num_segments=N)
    return y.reshape(B, T, D)


def setup_inputs(seed: int = 0) -> dict:
    key = jax.random.key(seed)
    ks = iter(jax.random.split(key, 40))

    def nrm(shape, scale):
        return jax.random.normal(next(ks), shape, jnp.float32) * scale

    def unif(shape, lo, hi):
        return jax.random.uniform(next(ks), shape, jnp.float32, lo, hi)

    L, D, W, S = DEPTH, D_MODEL, RWKV_WIDTH, SG_WIDTH
    return {
        'x': nrm((BATCH, SEQ, D), 1.0),
        'c': nrm((BATCH, D), 1.0),
        'w_mod': nrm((L, D, 6 * D), 0.5 * D ** -0.5),
        'b_mod': nrm((L, 6 * D), 0.01),
        'w_in': nrm((L, D, IN_COLS), D ** -0.5),
        'rwkv_mu': unif((L, RWKV_COLS), 0.0, 1.0),
        'w0': unif((L, W), -5.0, 0.0),
        'w2': nrm((L, DECAY_LORA, W), 0.5 * DECAY_LORA ** -0.5),
        'a0': nrm((L, W), 0.1),
        'a2': nrm((L, AAA_LORA, W), AAA_LORA ** -0.5),
        'g2': nrm((L, GATE_LORA, W), GATE_LORA ** -0.5),
        'k_k': 0.85 + nrm((L, W), 0.05),
        'k_a': 1.0 + nrm((L, W), 0.05),
        'r_k': nrm((L, RWKV_HEADS, RWKV_HEAD_DIM), 0.1),
        'gn_g': 1.0 + nrm((L, W), 0.05),
        'gn_b': nrm((L, W), 0.01),
        'vm0': nrm((L - 1, W), 0.1),
        'vm1': nrm((L - 1, W, MV_LORA), W ** -0.5),
        'vm2': nrm((L - 1, MV_LORA, W), MV_LORA ** -0.5),
        'sg_ln_g': 1.0 + nrm((L, S), 0.05),
        'sg_ln_b': nrm((L, S), 0.01),
        'w_s': nrm((L, SG_GROUPS, SG_CHUNK, SG_CHUNK), SG_CHUNK ** -0.5),
        'b_s': 1.0 + nrm((L, SG_GROUPS, SG_CHUNK), 0.05),
        'w_branch_a': nrm((L, W, D), W ** -0.5),
        'w_branch_b': nrm((L, S, D), S ** -0.5),
        'w_out': nrm((L, D, D), DEEPNORM_BETA * D ** -0.5),
        'ln1_g': 1.0 + nrm((L, D), 0.05),
        'ln1_b': nrm((L, D), 0.01),
        'router_w': nrm((D, N_EXPERTS), D ** -0.5),
        'router_b': nrm((N_EXPERTS,), 0.01),
        'e_gate': nrm((L, N_EXPERTS, D, D_FF_EXPERT), D ** -0.5),
        'e_up': nrm((L, N_EXPERTS, D, D_FF_EXPERT), D ** -0.5),
        'e_down': nrm((L, N_EXPERTS, D_FF_EXPERT, D), DEEPNORM_BETA * D_FF_EXPERT ** -0.5),
        'ln2_g': 1.0 + nrm((L, D), 0.05),
        'ln2_b': nrm((L, D), 0.01),
    }


def reference(x, c, w_mod, b_mod, w_in, rwkv_mu, w0, w2, a0, a2, g2, k_k, k_a, r_k, gn_g, gn_b,
              vm0, vm1, vm2, sg_ln_g, sg_ln_b, w_s, b_s, w_branch_a, w_branch_b, w_out, ln1_g, ln1_b,
              router_w, router_b, e_gate, e_up, e_down, ln2_g, ln2_b):
    cond = jax.nn.silu(c)
    v_first = None
    for i in range(DEPTH):
        mod = (cond @ w_mod[i] + b_mod[i])[:, None, :]
        shift1, scale1, gate1, shift2, scale2, gate2 = jnp.split(mod, 6, axis=-1)
        h = (_ln(x) * (1 + scale1) + shift1).astype(x.dtype)
        z = h @ w_in[i]
        z_rwkv = z[..., :RWKV_COLS]
        z_sg = z[..., RWKV_COLS:RWKV_COLS + SG_COLS]
        z_gate = z[..., RWKV_COLS + SG_COLS:]
        vmix = None if i == 0 else (vm0[i - 1], vm1[i - 1], vm2[i - 1])
        y_a, v_first = rwkv7_branch(z_rwkv, v_first, vmix, rwkv_mu[i], w0[i], w2[i], a0[i], a2[i], g2[i],
                                    k_k[i], k_a[i], r_k[i], gn_g[i], gn_b[i])
        y_b = spatial_gating_branch(z_sg, sg_ln_g[i], sg_ln_b[i], w_s[i], b_s[i])
        gate_a = jax.nn.sigmoid(z_gate[..., :D_MODEL])
        gate_b = jax.nn.sigmoid(z_gate[..., D_MODEL:])
        mixed = (gate_a * (y_a @ w_branch_a[i]) + gate_b * (y_b @ w_branch_b[i])) @ w_out[i]
        x = layer_norm(DEEPNORM_ALPHA * x + (1 + gate1) * mixed, ln1_g[i], ln1_b[i])
        h = (_ln(x) * (1 + scale2) + shift2).astype(x.dtype)
        ffn = moe_ffn(h, router_w, router_b, e_gate[i], e_up[i], e_down[i])
        x = layer_norm(DEEPNORM_ALPHA * x + (1 + gate2) * ffn, ln2_g[i], ln2_b[i])
    return x
```

```python
import functools

import jax
import jax.numpy as jnp
from jax import lax
from jax.experimental import pallas as pl
from jax.experimental.pallas import tpu as pltpu

LN_EPS = 1e-5
GN_EPS = 64e-5
N_EXPERT_GROUPS = 4
LANES = 128
SCAN_CHUNK = 64
HEAD_DIM = 64
VMEM_LIMIT = 56 * 1024 * 1024

_HI = lax.Precision.HIGHEST
_NT = (((1,), (1,)), ((), ()))
_TN = (((0,), (0,)), ((), ()))


def _bdot(a, b):
    return jnp.dot(a.astype(jnp.bfloat16), b.astype(jnp.bfloat16), preferred_element_type=jnp.float32)


def _split_dot(a, b_exact, terms=2):
    acc = None
    rem = a
    for _ in range(terms):
        piece = rem.astype(jnp.bfloat16)
        part = jnp.dot(piece, b_exact, preferred_element_type=jnp.float32)
        acc = part if acc is None else acc + part
        rem = rem - piece.astype(jnp.float32)
    return acc


def _sigmoid(x):
    return 1.0 / (1.0 + jnp.exp(-x))


def _softplus(x):
    return jnp.maximum(x, 0.0) + jnp.log(1.0 + jnp.exp(-jnp.abs(x)))


def _ln_rows(x, eps):
    mu = jnp.mean(x, axis=-1, keepdims=True)
    xc = x - mu
    var = jnp.mean(xc * xc, axis=-1, keepdims=True)
    return xc * lax.rsqrt(var + eps)


def _const_spec(shape):
    nd = len(shape)
    return pl.BlockSpec(shape, lambda *_: (0,) * nd, pipeline_mode=pl.Buffered(1))


def _mod_kernel(c_ref, w_ref, b_ref, o_ref):
    c = c_ref[...]
    cond = c * _sigmoid(c)
    o_ref[0] = jnp.sum(cond * w_ref[0], axis=0, keepdims=True) + b_ref[0]


def _modulation(c, w_mod, b_mod):
    L, D, D6 = w_mod.shape
    nb = D6 // D
    out = pl.pallas_call(
        _mod_kernel,
        grid=(L, nb),
        in_specs=[pl.BlockSpec((D, 1), lambda l, j: (0, 0)),
                  pl.BlockSpec((1, D, D), lambda l, j: (l, 0, j)),
                  pl.BlockSpec((1, 1, D), lambda l, j: (l, 0, j))],
        out_specs=pl.BlockSpec((1, 1, D), lambda l, j: (l, 0, j)),
        out_shape=jax.ShapeDtypeStruct((L, 1, D6), jnp.float32),
        compiler_params=pltpu.CompilerParams(dimension_semantics=("arbitrary", "arbitrary"),
                                             vmem_limit_bytes=VMEM_LIMIT),
        name="adaln_mod",
    )(c.reshape(D, 1), w_mod, b_mod.reshape(L, 1, D6))
    return out


def _front_kernel(has_vmix, W, *refs):
    (x_ref, sh_ref, sc_ref, wrw_ref, wsg_ref, wgt_ref, mu_ref, w0_ref, w2p_ref, a0_ref, a2p_ref, g2_ref,
     kkw_ref, ka_ref, hsum_ref) = refs[:15]
    pos = 15
    if has_vmix:
        vf_ref, vm0_ref, vm1_ref, vm2_ref = refs[pos:pos + 4]
        pos += 4
    lng_ref, lnb_ref, wpair_ref, sbias_ref, wbb_ref = refs[pos:pos + 5]
    pos += 5
    r_o, k_o, v_o, kk_o, b_o, lw_o, g_o, ga_o, mixb_o = refs[pos:pos + 9]
    carry_ref = refs[pos + 9]

    tm = x_ref.shape[0]

    @pl.when(pl.program_id(0) == 0)
    def _():
        carry_ref[...] = jnp.zeros_like(carry_ref)

    h = _ln_rows(x_ref[...], LN_EPS) * (1.0 + sc_ref[...]) + sh_ref[...]
    hb = h.astype(jnp.bfloat16)

    z = jnp.dot(hb, wrw_ref[...], preferred_element_type=jnp.float32)
    row = lax.broadcasted_iota(jnp.int32, z.shape, 0)
    prev = jnp.where(row == 0, carry_ref[...], pltpu.roll(z, 1, axis=0))
    carry_ref[...] = z[tm - 1:tm, :]
    z = z + (prev - z) * mu_ref[...]
    r = z[:, :W]
    k = z[:, W:2 * W]
    v = z[:, 2 * W:3 * W]
    zwa = z[:, 3 * W:3 * W + LANES]
    zg = z[:, 3 * W + LANES:]
    w_log = -_softplus(-(w0_ref[...] + _bdot(jnp.tanh(zwa), w2p_ref[...]))) - 0.5
    lw_o[...] = -jnp.exp(w_log)
    a = _sigmoid(a0_ref[...] + _bdot(zwa, a2p_ref[...]))
    g_o[...] = _bdot(_sigmoid(zg), g2_ref[...])
    if has_vmix:
        mix = _sigmoid(vm0_ref[...] + _bdot(_bdot(v, vm1_ref[...]), vm2_ref[...]))
        v = v + (vf_ref[...] - v) * mix
    kk = k * kkw_ref[...]
    ss = _split_dot(kk * kk, hsum_ref[...])
    kk = kk / jnp.maximum(jnp.sqrt(ss), 1e-12)
    r_o[...] = r
    k_o[...] = k * (1.0 + (a - 1.0) * ka_ref[...])
    v_o[...] = v
    kk_o[...] = kk
    b_o[...] = kk * a

    zs = jnp.dot(hb, wsg_ref[...], preferred_element_type=jnp.float32)
    zs = zs * (0.5 * (1.0 + jnp.tanh(0.7978845608028654 * (zs + 0.044715 * (zs * zs * zs)))))
    S = zs.shape[1] // 2
    u = zs[:, :S]
    vln = _ln_rows(zs[:, S:], LN_EPS) * lng_ref[...] + lnb_ref[...]
    lo = lax.broadcasted_iota(jnp.int32, (LANES, LANES), 1) < HEAD_DIM
    chunks = []
    for c in range(tm // LANES):
        cols = []
        for p in range(S // LANES):
            vp = vln[c * LANES:(c + 1) * LANES, p * LANES:(p + 1) * LANES]
            vs = jnp.concatenate([jnp.where(lo, vp, 0.0), jnp.where(lo, 0.0, vp)], axis=0)
            cols.append(_bdot(wpair_ref[p], vs))
        chunks.append(jnp.concatenate(cols, axis=1) + sbias_ref[...])
    yb = u * jnp.concatenate(chunks, axis=0)

    zg2 = jnp.dot(hb, wgt_ref[...], preferred_element_type=jnp.float32)
    Dm = zg2.shape[1] // 2
    ga_o[...] = _sigmoid(zg2[:, :Dm])
    mixb_o[...] = _sigmoid(zg2[:, Dm:]) * _bdot(yb, wbb_ref[...])


def _front(x, shift, scale, p, v_first, tm):
    T, D = x.shape
    W = p["hsum"].shape[0]
    has_vmix = v_first is not None
    row = lambda w: pl.BlockSpec((tm, w), lambda i: (i, 0))
    ins = [x, shift, scale, p["w_rw"], p["w_sg"], p["w_gt"], p["mu"], p["w0"], p["w2p"], p["a0"], p["a2p"], p["g2"],
           p["k_k"], p["k_a"], p["hsum"]]
    specs = [row(D)] + [_const_spec(a.shape) for a in ins[1:]]
    if has_vmix:
        extra = [v_first, p["vm0"], p["vm1"], p["vm2"]]
        ins += extra
        specs += [row(W)] + [_const_spec(a.shape) for a in extra[1:]]
    tail = [p["sg_ln_g"], p["sg_ln_b"], p["wpair"], p["sbias"], p["w_bb"]]
    ins += tail
    specs += [_const_spec(a.shape) for a in tail]
    f32 = jnp.float32
    out_shape = [jax.ShapeDtypeStruct((T, W), f32)] * 7 + [jax.ShapeDtypeStruct((T, D), f32)] * 2
    out_specs = [row(W)] * 7 + [row(D)] * 2
    return pl.pallas_call(
        functools.partial(_front_kernel, has_vmix, W),
        grid=(T // tm,),
        in_specs=specs,
        out_specs=out_specs,
        out_shape=out_shape,
        scratch_shapes=[pltpu.VMEM((1, p["w_rw"].shape[1]), f32)],
        compiler_params=pltpu.CompilerParams(dimension_semantics=("arbitrary",), vmem_limit_bytes=VMEM_LIMIT),
        name="mixer_front",
    )(*ins)


def _sdot(a, b, dims=(((1,), (0,)), ((), ()))):
    return lax.dot_general(a, b, dims, precision=_HI, preferred_element_type=jnp.float32)


def _scan_kernel(r_ref, k_ref, v_ref, kk_ref, b_ref, lw_ref, y_ref, s_ref):
    C = SCAN_CHUNK
    P = LANES

    @pl.when(pl.program_id(0) == 0)
    def _():
        s_ref[...] = jnp.zeros_like(s_ref)

    lw = lw_ref[...]
    tri = (lax.broadcasted_iota(jnp.int32, (C, C), 0) >= lax.broadcasted_iota(jnp.int32, (C, C), 1))
    tri_b = jnp.where(tri, 1.0, 0.0).astype(jnp.bfloat16)
    cum = None
    rem = lw
    for _ in range(3):
        piece = rem.astype(jnp.bfloat16)
        part = jnp.dot(tri_b, piece, preferred_element_type=jnp.float32)
        cum = part if cum is None else cum + part
        rem = rem - piece.astype(jnp.float32)
    cum_end = cum[C - 1:C, :]
    w_t = jnp.exp(cum)
    w_inv = jnp.exp(-cum)
    w_prev = jnp.exp(cum - lw)
    w_end = jnp.exp(cum_end - cum)
    w_tot = jnp.exp(cum_end)

    kk = kk_ref[...]
    bb = b_ref[...]
    kx = k_ref[...]
    a_t = -kk * w_prev
    b_t = bb * w_inv
    k_t = kx * w_inv
    r_t = r_ref[...] * w_t
    b_h = bb * w_end
    k_h = kx * w_end
    vv = v_ref[...]

    ri = lax.broadcasted_iota(jnp.int32, (P, P), 0)
    ci = lax.broadcasted_iota(jnp.int32, (P, P), 1)
    same = (ri >= C) == (ci >= C)
    strict = jnp.where(same & (ri > ci), 1.0, 0.0)
    incl = jnp.where(same & (ri >= ci), 1.0, 0.0)
    bdiag = jnp.where(same, 1.0, 0.0)
    eye = ri == ci
    lo = lax.broadcasted_iota(jnp.int32, (C, P), 1) < HEAD_DIM

    def stack(m):
        return jnp.concatenate([jnp.where(lo, m, 0.0), jnp.where(lo, 0.0, m)], axis=0)

    def fold(m):
        return m[:C] + m[C:]

    for p in range(lw.shape[1] // P):
        sl = slice(p * P, (p + 1) * P)
        a_s, r_s, v_s = stack(a_t[:, sl]), stack(r_t[:, sl]), stack(vv[:, sl])
        lhs = jnp.concatenate([a_s, r_s], axis=0)
        rhs = jnp.concatenate([b_t[:, sl], b_t[:, sl], k_t[:, sl], k_t[:, sl]], axis=0)
        q = _sdot(lhs, rhs, _NT)
        l_ab = q[:P, :P] * strict
        l_ak = q[:P, P:] * strict
        p_rb = q[P:, :P] * incl
        p_rk = q[P:, P:] * incl
        inv = jnp.where(eye, 1.0, 0.0) + jnp.where(((ri >> 1) == (ci >> 1)), l_ab, 0.0)
        for lg in range(1, 6):
            off = ((ri >> (lg + 1)) == (ci >> (lg + 1))) & (((ri >> lg) & 1) == 1) & (((ci >> lg) & 1) == 0)
            inv = inv + _sdot(inv, _sdot(jnp.where(off, l_ab, 0.0), inv))
        lak_v = _sdot(l_ak, v_s)
        prk_v = _sdot(p_rk, v_s)
        ta = _sdot(inv, jnp.concatenate([a_s, lak_v], axis=1))
        pa = _sdot(p_rb, ta) + jnp.concatenate([r_s, prk_v], axis=1)
        a_p, u_v = fold(ta[:, :P]), fold(ta[:, P:])
        r_p, y_loc = fold(pa[:, :P]), fold(pa[:, P:])
        g_m = _sdot(b_h[:, sl], a_p, _TN) * bdiag + jnp.where(eye, w_tot[:, sl], 0.0)
        h_m = (_sdot(b_h[:, sl], u_v, _TN) + _sdot(k_h[:, sl], vv[:, sl], _TN)) * bdiag
        s_old = s_ref[p]
        y_ref[:, sl] = _sdot(r_p, s_old) + y_loc
        s_ref[p] = _sdot(g_m, s_old) + h_m


def _scan(r, k, v, kk, b, lw):
    T, W = r.shape
    C = SCAN_CHUNK
    spec = pl.BlockSpec((C, W), lambda i: (i, 0))
    return pl.pallas_call(
        _scan_kernel,
        grid=(T // C,),
        in_specs=[spec] * 6,
        out_specs=spec,
        out_shape=jax.ShapeDtypeStruct((T, W), jnp.float32),
        scratch_shapes=[pltpu.VMEM((W // LANES, LANES, LANES), jnp.float32)],
        compiler_params=pltpu.CompilerParams(dimension_semantics=("arbitrary",), vmem_limit_bytes=VMEM_LIMIT),
        name="wkv7_scan",
    )(r, k, v, kk, b, lw)


def _post_kernel(alpha, y_ref, r_ref, k_ref, v_ref, g_ref, ga_ref, mixb_ref, x_ref, rk_ref, gng_ref, gnb_ref,
                 hsum_ref, wba_ref, wout_ref, gate_ref, lng_ref, lnb_ref, o_ref):
    y = y_ref[...]
    hs = hsum_ref[...]
    inv_n = 1.0 / HEAD_DIM
    mu = _split_dot(y, hs) * inv_n
    yc = y - mu
    var = _split_dot(yc * yc, hs) * inv_n
    yn = yc * lax.rsqrt(var + GN_EPS) * gng_ref[...] + gnb_ref[...]
    v = v_ref[...]
    bonus = _split_dot(r_ref[...] * k_ref[...] * rk_ref[...], hs)
    ya = (yn + bonus * v) * g_ref[...]
    mixed = ga_ref[...] * _bdot(ya, wba_ref[...]) + mixb_ref[...]
    mixed = _bdot(mixed, wout_ref[...])
    xn = alpha * x_ref[...] + (1.0 + gate_ref[...]) * mixed
    o_ref[...] = _ln_rows(xn, LN_EPS) * lng_ref[...] + lnb_ref[...]


def _post(alpha, y, r, k, v, g, ga, mixb, x, gate, p, tm):
    T, D = x.shape
    W = y.shape[1]
    row = lambda w: pl.BlockSpec((tm, w), lambda i: (i, 0))
    consts = [p["r_k"], p["gn_g"], p["gn_b"], p["hsum"], p["w_ba"], p["w_out"], gate, p["ln1_g"], p["ln1_b"]]
    return pl.pallas_call(
        functools.partial(_post_kernel, alpha),
        grid=(T // tm,),
        in_specs=[row(W)] * 5 + [row(D)] * 3 + [_const_spec(a.shape) for a in consts],
        out_specs=row(D),
        out_shape=jax.ShapeDtypeStruct((T, D), jnp.float32),
        compiler_params=pltpu.CompilerParams(dimension_semantics=("arbitrary",), vmem_limit_bytes=VMEM_LIMIT),
        name="mixer_post",
    )(y, r, k, v, g, ga, mixb, x, *consts)


def _route_kernel(x_ref, sh_ref, sc_ref, rwt_ref, rb_ref, h_ref, wt_ref):
    h = _ln_rows(x_ref[...], LN_EPS) * (1.0 + sc_ref[...]) + sh_ref[...]
    h_ref[...] = h.astype(jnp.bfloat16)
    logits = lax.dot_general(rwt_ref[...], h, _NT, precision=_HI, preferred_element_type=jnp.float32)
    scores = _sigmoid(logits)
    sel = scores + rb_ref[...]
    E = sel.shape[0]
    per = E // N_EXPERT_GROUPS
    rows = [sel[e:e + 1, :] for e in range(E)]
    gscore = []
    for g in range(N_EXPERT_GROUPS):
        m = rows[g * per:(g + 1) * per]
        best = None
        for i in range(per):
            for j in range(i + 1, per):
                s2 = m[i] + m[j]
                best = s2 if best is None else jnp.maximum(best, s2)
        gscore.append(best)
    cur = gscore[0]
    bestg = jnp.zeros(cur.shape, jnp.int32)
    for g in range(1, N_EXPERT_GROUPS):
        take = gscore[g] > cur
        cur = jnp.where(take, gscore[g], cur)
        bestg = jnp.where(take, g, bestg)
    picked = []
    for e in range(E):
        g = e // per
        rank = jnp.zeros(cur.shape, jnp.int32)
        for j in range(g * per, (g + 1) * per):
            if j == e:
                continue
            ahead = (rows[j] > rows[e]) | ((rows[j] == rows[e]) & (j < e))
            rank = rank + ahead.astype(jnp.int32)
        picked.append(jnp.where((bestg == g) & (rank < 2), scores[e:e + 1, :], 0.0))
    total = picked[0]
    for e in range(1, E):
        total = total + picked[e]
    wt_ref[...] = jnp.concatenate(picked, axis=0) / total


def _route(x, shift, scale, router_wt, router_b, tm):
    T, D = x.shape
    E = router_wt.shape[0]
    return pl.pallas_call(
        _route_kernel,
        grid=(T // tm,),
        in_specs=[pl.BlockSpec((tm, D), lambda i: (i, 0)), _const_spec(shift.shape), _const_spec(scale.shape),
                  _const_spec(router_wt.shape), _const_spec(router_b.shape)],
        out_specs=[pl.BlockSpec((tm, D), lambda i: (i, 0)), pl.BlockSpec((E, tm), lambda i: (0, i))],
        out_shape=[jax.ShapeDtypeStruct((T, D), jnp.bfloat16), jax.ShapeDtypeStruct((E, T), jnp.float32)],
        compiler_params=pltpu.CompilerParams(dimension_semantics=("arbitrary",), vmem_limit_bytes=VMEM_LIMIT),
        name="moe_route",
    )(x, shift, scale, router_wt, router_b)


def _moe_kernel(alpha, x_ref, h_ref, wt_ref, eg_ref, eu_ref, ed_ref, gate_ref, lng_ref, lnb_ref, o_ref, acc_ref):
    e = pl.program_id(1)

    @pl.when(e == 0)
    def _():
        acc_ref[...] = jnp.zeros_like(acc_ref)

    h = h_ref[...]
    wt = wt_ref[...]
    lane = lax.broadcasted_iota(jnp.int32, wt.shape, 1)
    wcol = jnp.sum(jnp.where(lane == e, wt, 0.0), axis=-1, keepdims=True)
    gt = jnp.dot(h, eg_ref[0], preferred_element_type=jnp.float32)
    up = jnp.dot(h, eu_ref[0], preferred_element_type=jnp.float32)
    hid = gt * _sigmoid(gt) * up * wcol
    acc_ref[...] += jnp.dot(hid.astype(jnp.bfloat16), ed_ref[0], preferred_element_type=jnp.float32)

    @pl.when(e == pl.num_programs(1) - 1)
    def _():
        xn = alpha * x_ref[...] + (1.0 + gate_ref[...]) * acc_ref[...]
        o_ref[...] = _ln_rows(xn, LN_EPS) * lng_ref[...] + lnb_ref[...]


def _moe(alpha, x, h, wt, eg, eu, ed, gate, ln_g, ln_b, tm):
    T, D = x.shape
    E, _, F = eg.shape
    return pl.pallas_call(
        functools.partial(_moe_kernel, alpha),
        grid=(T // tm, E),
        in_specs=[pl.BlockSpec((tm, D), lambda i, e: (i, 0)),
                  pl.BlockSpec((tm, D), lambda i, e: (i, 0)),
                  pl.BlockSpec((tm, E), lambda i, e: (i, 0)),
                  pl.BlockSpec((1, D, F), lambda i, e: (e, 0, 0)),
                  pl.BlockSpec((1, D, F), lambda i, e: (e, 0, 0)),
                  pl.BlockSpec((1, F, D), lambda i, e: (e, 0, 0)),
                  pl.BlockSpec((1, D), lambda i, e: (0, 0)),
                  pl.BlockSpec((1, D), lambda i, e: (0, 0)),
                  pl.BlockSpec((1, D), lambda i, e: (0, 0))],
        out_specs=pl.BlockSpec((tm, D), lambda i, e: (i, 0)),
        out_shape=jax.ShapeDtypeStruct((T, D), jnp.float32),
        scratch_shapes=[pltpu.VMEM((tm, D), jnp.float32)],
        compiler_params=pltpu.CompilerParams(dimension_semantics=("arbitrary", "arbitrary"),
                                             vmem_limit_bytes=VMEM_LIMIT),
        name="moe_experts",
    )(x, h, wt, eg, eu, ed, gate, ln_g, ln_b)


def kernel(x, c, w_mod, b_mod, w_in, rwkv_mu, w0, w2, a0, a2, g2, k_k, k_a, r_k, gn_g, gn_b, vm0, vm1, vm2,
           sg_ln_g, sg_ln_b, w_s, b_s, w_branch_a, w_branch_b, w_out, ln1_g, ln1_b, router_w, router_b,
           e_gate, e_up, e_down, ln2_g, ln2_b):
    B, T, D = x.shape
    assert B == 1
    L = w_mod.shape[0]
    W = w0.shape[1]
    S = sg_ln_g.shape[1]
    dl, al = w2.shape[1], a2.shape[1]
    assert dl + al == LANES and W % LANES == 0 and S % LANES == 0
    rw_cols = rwkv_mu.shape[1]
    G, CH = w_s.shape[1], w_s.shape[2]
    assert CH == LANES and S // G == HEAD_DIM and W // r_k.shape[1] == HEAD_DIM
    alpha = float((2 * L) ** 0.25)
    bf = jnp.bfloat16
    f32 = jnp.float32

    mod = _modulation(c, w_mod, b_mod)
    head = jnp.arange(W) // HEAD_DIM
    hsum = (head[:, None] == head[None, :]).astype(bf)
    causal = jnp.tril(jnp.ones((CH, CH), bool))
    router_wt = router_w.T
    router_bc = router_b.reshape(-1, 1)

    xt = x.reshape(T, D)
    v_first = None
    for i in range(L):
        sh1, sc1, gt1, sh2, sc2, gt2 = [mod[i, :, j * D:(j + 1) * D] for j in range(6)]
        w_m = jnp.where(causal, w_s[i], 0.0)
        p = dict(
            w_rw=w_in[i, :, :rw_cols].astype(bf),
            w_sg=w_in[i, :, rw_cols:rw_cols + 2 * S].astype(bf),
            w_gt=w_in[i, :, rw_cols + 2 * S:].astype(bf),
            mu=rwkv_mu[i][None], w0=w0[i][None], a0=a0[i][None],
            w2p=jnp.concatenate([w2[i], jnp.zeros((al, W), f32)], 0).astype(bf),
            a2p=jnp.concatenate([jnp.zeros((dl, W), f32), a2[i]], 0).astype(bf),
            g2=g2[i].astype(bf), k_k=k_k[i][None], k_a=k_a[i][None], hsum=hsum,
            sg_ln_g=sg_ln_g[i][None], sg_ln_b=sg_ln_b[i][None],
            wpair=jnp.concatenate([w_m[0::2], w_m[1::2]], axis=2).astype(bf),
            sbias=jnp.repeat(b_s[i].T, HEAD_DIM, axis=1),
            w_bb=w_branch_b[i].astype(bf), w_ba=w_branch_a[i].astype(bf), w_out=w_out[i].astype(bf),
            r_k=r_k[i].reshape(1, W), gn_g=gn_g[i][None], gn_b=gn_b[i][None],
            ln1_g=ln1_g[i][None], ln1_b=ln1_b[i][None],
        )
        if i > 0:
            p.update(vm0=vm0[i - 1][None], vm1=vm1[i - 1].astype(bf), vm2=vm2[i - 1].astype(bf))
        r, k, v, kk, b, lw, g, ga, mixb = _front(xt, sh1, sc1, p, v_first, tm=256)
        if i == 0:
            v_first = v
        y = _scan(r, k, v, kk, b, lw)
        xt = _post(alpha, y, r, k, v, g, ga, mixb, xt, gt1, p, tm=256)
        h, wt = _route(xt, sh2, sc2, router_wt, router_bc, tm=512)
        xt = _moe(alpha, xt, h, wt.T, e_gate[i].astype(bf), e_up[i].astype(bf), e_down[i].astype(bf),
                  gt2, ln2_g[i][None], ln2_b[i][None], tm=1024)
    return xt.reshape(B, T, D)
```

```python
import functools

import jax
import jax.numpy as jnp
from jax import lax
from jax.experimental import pallas as pl
from jax.experimental.pallas import tpu as pltpu

LN_EPS = 1e-5
GN_EPS = 64e-5
N_EXPERT_GROUPS = 4
LANES = 128
SCAN_CHUNK = 64
HEAD_DIM = 64
VMEM_LIMIT = 56 * 1024 * 1024

_HI = lax.Precision.HIGHEST
_NT = (((1,), (1,)), ((), ()))
_TN = (((0,), (0,)), ((), ()))


def _bdot(a, b):
    return jnp.dot(a.astype(jnp.bfloat16), b.astype(jnp.bfloat16), preferred_element_type=jnp.float32)


def _split_dot(a, b_exact, terms=2):
    acc = None
    rem = a
    for _ in range(terms):
        piece = rem.astype(jnp.bfloat16)
        part = jnp.dot(piece, b_exact, preferred_element_type=jnp.float32)
        acc = part if acc is None else acc + part
        rem = rem - piece.astype(jnp.float32)
    return acc


def _sigmoid(x):
    return 1.0 / (1.0 + jnp.exp(-x))


def _softplus(x):
    return jnp.maximum(x, 0.0) + jnp.log(1.0 + jnp.exp(-jnp.abs(x)))


def _ln_rows(x, eps):
    mu = jnp.mean(x, axis=-1, keepdims=True)
    xc = x - mu
    var = jnp.mean(xc * xc, axis=-1, keepdims=True)
    return xc * lax.rsqrt(var + eps)


def _const_spec(shape):
    nd = len(shape)
    return pl.BlockSpec(shape, lambda *_: (0,) * nd, pipeline_mode=pl.Buffered(1))


def _mod_kernel(c_ref, w_ref, b_ref, o_ref):
    c = c_ref[...]
    cond = c * _sigmoid(c)
    o_ref[0] = jnp.sum(cond * w_ref[0], axis=0, keepdims=True) + b_ref[0]


def _modulation(c, w_mod, b_mod):
    L, D, D6 = w_mod.shape
    nb = D6 // D
    out = pl.pallas_call(
        _mod_kernel,
        grid=(L, nb),
        in_specs=[pl.BlockSpec((D, 1), lambda l, j: (0, 0)),
                  pl.BlockSpec((1, D, D), lambda l, j: (l, 0, j)),
                  pl.BlockSpec((1, 1, D), lambda l, j: (l, 0, j))],
        out_specs=pl.BlockSpec((1, 1, D), lambda l, j: (l, 0, j)),
        out_shape=jax.ShapeDtypeStruct((L, 1, D6), jnp.float32),
        compiler_params=pltpu.CompilerParams(dimension_semantics=("arbitrary", "arbitrary"),
                                             vmem_limit_bytes=VMEM_LIMIT),
        name="adaln_mod",
    )(c.reshape(D, 1), w_mod, b_mod.reshape(L, 1, D6))
    return out


def _front_kernel(has_vmix, W, *refs):
    (x_ref, sh_ref, sc_ref, wrw_ref, wsg_ref, wgt_ref, mu_ref, w0_ref, w2p_ref, a0_ref, a2p_ref, g2_ref,
     kkw_ref, ka_ref, hsum_ref) = refs[:15]
    pos = 15
    if has_vmix:
        vf_ref, vm0_ref, vm1_ref, vm2_ref = refs[pos:pos + 4]
        pos += 4
    lng_ref, lnb_ref, wpair_ref, sbias_ref, wbb_ref = refs[pos:pos + 5]
    pos += 5
    r_o, k_o, v_o, kk_o, b_o, lw_o, g_o, ga_o, mixb_o = refs[pos:pos + 9]
    carry_ref = refs[pos + 9]

    tm = x_ref.shape[0]

    @pl.when(pl.program_id(0) == 0)
    def _():
        carry_ref[...] = jnp.zeros_like(carry_ref)

    h = _ln_rows(x_ref[...], LN_EPS) * (1.0 + sc_ref[...]) + sh_ref[...]
    hb = h.astype(jnp.bfloat16)

    z = jnp.dot(hb, wrw_ref[...], preferred_element_type=jnp.float32)
    row = lax.broadcasted_iota(jnp.int32, z.shape, 0)
    prev = jnp.where(row == 0, carry_ref[...], pltpu.roll(z, 1, axis=0))
    carry_ref[...] = z[tm - 1:tm, :]
    z = z + (prev - z) * mu_ref[...]
    r = z[:, :W]
    k = z[:, W:2 * W]
    v = z[:, 2 * W:3 * W]
    zwa = z[:, 3 * W:3 * W + LANES]
    zg = z[:, 3 * W + LANES:]
    w_log = -_softplus(-(w0_ref[...] + _bdot(jnp.tanh(zwa), w2p_ref[...]))) - 0.5
    lw_o[...] = -jnp.exp(w_log)
    a = _sigmoid(a0_ref[...] + _bdot(zwa, a2p_ref[...]))
    g_o[...] = _bdot(_sigmoid(zg), g2_ref[...])
    if has_vmix:
        mix = _sigmoid(vm0_ref[...] + _bdot(_bdot(v, vm1_ref[...]), vm2_ref[...]))
        v = v + (vf_ref[...] - v) * mix
    kk = k * kkw_ref[...]
    ss = _split_dot(kk * kk, hsum_ref[...])
    kk = kk / jnp.maximum(jnp.sqrt(ss), 1e-12)
    r_o[...] = r
    k_o[...] = k * (1.0 + (a - 1.0) * ka_ref[...])
    v_o[...] = v
    kk_o[...] = kk
    b_o[...] = kk * a

    zs = jnp.dot(hb, wsg_ref[...], preferred_element_type=jnp.float32)
    zs = zs * (0.5 * (1.0 + jnp.tanh(0.7978845608028654 * (zs + 0.044715 * (zs * zs * zs)))))
    S = zs.shape[1] // 2
    u = zs[:, :S]
    vln = _ln_rows(zs[:, S:], LN_EPS) * lng_ref[...] + lnb_ref[...]
    lo = lax.broadcasted_iota(jnp.int32, (LANES, LANES), 1) < HEAD_DIM
    chunks = []
    for c in range(tm // LANES):
        cols = []
        for p in range(S // LANES):
            vp = vln[c * LANES:(c + 1) * LANES, p * LANES:(p + 1) * LANES]
            vs = jnp.concatenate([jnp.where(lo, vp, 0.0), jnp.where(lo, 0.0, vp)], axis=0)
            cols.append(_bdot(wpair_ref[p], vs))
        chunks.append(jnp.concatenate(cols, axis=1) + sbias_ref[...])
    yb = u * jnp.concatenate(chunks, axis=0)

    zg2 = jnp.dot(hb, wgt_ref[...], preferred_element_type=jnp.float32)
    Dm = zg2.shape[1] // 2
    ga_o[...] = _sigmoid(zg2[:, :Dm])
    mixb_o[...] = _sigmoid(zg2[:, Dm:]) * _bdot(yb, wbb_ref[...])


def _front(x, shift, scale, p, v_first, tm):
    T, D = x.shape
    W = p["hsum"].shape[0]
    has_vmix = v_first is not None
    row = lambda w: pl.BlockSpec((tm, w), lambda i: (i, 0))
    ins = [x, shift, scale, p["w_rw"], p["w_sg"], p["w_gt"], p["mu"], p["w0"], p["w2p"], p["a0"], p["a2p"], p["g2"],
           p["k_k"], p["k_a"], p["hsum"]]
    specs = [row(D)] + [_const_spec(a.shape) for a in ins[1:]]
    if has_vmix:
        extra = [v_first, p["vm0"], p["vm1"], p["vm2"]]
        ins += extra
        specs += [row(W)] + [_const_spec(a.shape) for a in extra[1:]]
    tail = [p["sg_ln_g"], p["sg_ln_b"], p["wpair"], p["sbias"], p["w_bb"]]
    ins += tail
    specs += [_const_spec(a.shape) for a in tail]
    f32 = jnp.float32
    out_shape = [jax.ShapeDtypeStruct((T, W), f32)] * 7 + [jax.ShapeDtypeStruct((T, D), f32)] * 2
    out_specs = [row(W)] * 7 + [row(D)] * 2
    return pl.pallas_call(
        functools.partial(_front_kernel, has_vmix, W),
        grid=(T // tm,),
        in_specs=specs,
        out_specs=out_specs,
        out_shape=out_shape,
        scratch_shapes=[pltpu.VMEM((1, p["w_rw"].shape[1]), f32)],
        compiler_params=pltpu.CompilerParams(dimension_semantics=("arbitrary",), vmem_limit_bytes=VMEM_LIMIT),
        name="mixer_front",
    )(*ins)


def _scan_kernel(r_ref, k_ref, v_ref, kk_ref, b_ref, lw_ref, y_ref, s_ref):
    C = SCAN_CHUNK
    P = LANES
    bf = jnp.bfloat16
    tt, width = lw_ref.shape
    n_chunks = tt // C
    n_blocks = width // P

    @pl.when(pl.program_id(0) == 0)
    def _():
        s_ref[...] = jnp.zeros_like(s_ref)

    lw = lw_ref[...]
    ti = lax.broadcasted_iota(jnp.int32, (tt, tt), 0)
    tj = lax.broadcasted_iota(jnp.int32, (tt, tt), 1)
    same_chunk = (ti // C) == (tj // C)
    pre = jnp.concatenate([jnp.where(same_chunk & (ti >= tj), 1.0, 0.0),
                           jnp.where(same_chunk, 1.0, 0.0)], axis=0).astype(bf)
    sums = None
    rem = lw
    for _ in range(3):
        piece = rem.astype(bf)
        part = jnp.dot(pre, piece, preferred_element_type=jnp.float32)
        sums = part if sums is None else sums + part
        rem = rem - piece.astype(jnp.float32)
    cum, tot = sums[:tt], sums[tt:]
    w_inv = jnp.exp(-cum)
    w_end = jnp.exp(tot - cum)
    w_tot = jnp.exp(tot)

    kk = kk_ref[...]
    bb = b_ref[...]
    kx = k_ref[...]
    a_t = -kk * jnp.exp(cum - lw)
    b_t = (bb * w_inv).astype(bf)
    k_t = (kx * w_inv).astype(bf)
    r_t = r_ref[...] * jnp.exp(cum)
    b_h = (bb * w_end).astype(bf)
    k_h = (kx * w_end).astype(bf)
    vv = v_ref[...]

    ri = lax.broadcasted_iota(jnp.int32, (P, P), 0)
    ci = lax.broadcasted_iota(jnp.int32, (P, P), 1)
    same = (ri >= C) == (ci >= C)
    strict = same & (ri > ci)
    incl = same & (ri >= ci)
    eye = ri == ci
    lo = lax.broadcasted_iota(jnp.int32, (C, P), 1) < HEAD_DIM

    def stack(m):
        return jnp.concatenate([jnp.where(lo, m, 0.0), jnp.where(lo, 0.0, m)], axis=0)

    def fold(m):
        return m[:C] + m[C:]

    def mm(a, b, dims=(((1,), (0,)), ((), ()))):
        return lax.dot_general(a, b, dims, preferred_element_type=jnp.float32)

    chains = [(j, p) for j in range(n_chunks) for p in range(n_blocks)]
    n = len(chains)
    cut = lambda m, j, p: m[j * C:(j + 1) * C, p * P:(p + 1) * P]
    a_s = [stack(cut(a_t, j, p)).astype(bf) for j, p in chains]
    r_s = [stack(cut(r_t, j, p)) for j, p in chains]
    v_s = [stack(cut(vv, j, p)).astype(bf) for j, p in chains]
    q = [mm(jnp.concatenate([a_s[i], r_s[i].astype(bf)], axis=0),
            jnp.concatenate([cut(b_t, j, p)] * 2 + [cut(k_t, j, p)] * 2, axis=0), _NT)
         for i, (j, p) in enumerate(chains)]
    l_ab = [jnp.where(strict, m[:P, :P], 0.0) for m in q]
    low = [jnp.concatenate([jnp.where(strict, m[:P, P:], 0.0), jnp.where(incl, m[P:, P:], 0.0)], axis=0).astype(bf)
           for m in q]
    p_rb = [jnp.where(incl, m[P:, :P], 0.0).astype(bf) for m in q]
    inv = [jnp.where(eye, 1.0, jnp.where((ri >> 1) == (ci >> 1), m, 0.0)) for m in l_ab]
    for lg in range(1, 6):
        off = ((ri >> (lg + 1)) == (ci >> (lg + 1))) & (((ri >> lg) & 1) == 1) & (((ci >> lg) & 1) == 0)
        inv_b = [m.astype(bf) for m in inv]
        tmp = [mm(jnp.where(off, l_ab[i], 0.0).astype(bf), inv_b[i]) for i in range(n)]
        inv = [inv[i] + mm(inv_b[i], tmp[i].astype(bf)) for i in range(n)]
    lv = [mm(low[i], v_s[i]) for i in range(n)]
    ta = [mm(inv[i].astype(bf), jnp.concatenate([a_s[i], lv[i][:P].astype(bf)], axis=1))
          for i in range(n)]
    pa = [mm(p_rb[i], ta[i].astype(bf)) + jnp.concatenate([r_s[i], lv[i][P:]], axis=1) for i in range(n)]
    a_p = [fold(m[:, :P]) for m in ta]
    u_v = [fold(m[:, P:]) for m in ta]
    r_p = [fold(m[:, :P]).astype(bf) for m in pa]
    y_loc = [fold(m[:, P:]) for m in pa]
    zero = jnp.zeros((C, P), jnp.float32)
    gh = [mm(jnp.concatenate([cut(b_h, j, p), cut(k_h, j, p)], axis=0),
             jnp.concatenate([jnp.concatenate([a_p[i], u_v[i]], axis=1),
                              jnp.concatenate([zero, cut(vv, j, p)], axis=1)], axis=0).astype(bf), _TN)
          for i, (j, p) in enumerate(chains)]
    g_m = [(jnp.where(same, gh[i][:, :P], 0.0) + jnp.where(eye, cut(w_tot, j, p)[:1], 0.0)).astype(bf)
           for i, (j, p) in enumerate(chains)]
    h_m = [jnp.where(same, m[:, P:], 0.0) for m in gh]
    state = [s_ref[p] for p in range(n_blocks)]
    for i, (j, p) in enumerate(chains):
        s_b = state[p].astype(bf)
        y_ref[j * C:(j + 1) * C, p * P:(p + 1) * P] = mm(r_p[i], s_b) + y_loc[i]
        state[p] = mm(g_m[i], s_b) + h_m[i]
    for p in range(n_blocks):
        s_ref[p] = state[p]


def _scan(r, k, v, kk, b, lw, tt):
    T, W = r.shape
    spec = pl.BlockSpec((tt, W), lambda i: (i, 0))
    return pl.pallas_call(
        _scan_kernel,
        grid=(T // tt,),
        in_specs=[spec] * 6,
        out_specs=spec,
        out_shape=jax.ShapeDtypeStruct((T, W), jnp.float32),
        scratch_shapes=[pltpu.VMEM((W // LANES, LANES, LANES), jnp.float32)],
        compiler_params=pltpu.CompilerParams(dimension_semantics=("arbitrary",), vmem_limit_bytes=VMEM_LIMIT),
        name="wkv7_scan",
    )(r, k, v, kk, b, lw)


def _post_kernel(alpha, y_ref, r_ref, k_ref, v_ref, g_ref, ga_ref, mixb_ref, x_ref, rk_ref, gng_ref, gnb_ref,
                 hsum_ref, wba_ref, wout_ref, gate_ref, lng_ref, lnb_ref, o_ref):
    y = y_ref[...]
    hs = hsum_ref[...]
    inv_n = 1.0 / HEAD_DIM
    mu = _split_dot(y, hs) * inv_n
    yc = y - mu
    var = _split_dot(yc * yc, hs) * inv_n
    yn = yc * lax.rsqrt(var + GN_EPS) * gng_ref[...] + gnb_ref[...]
    v = v_ref[...]
    bonus = _split_dot(r_ref[...] * k_ref[...] * rk_ref[...], hs)
    ya = (yn + bonus * v) * g_ref[...]
    mixed = ga_ref[...] * _bdot(ya, wba_ref[...]) + mixb_ref[...]
    mixed = _bdot(mixed, wout_ref[...])
    xn = alpha * x_ref[...] + (1.0 + gate_ref[...]) * mixed
    o_ref[...] = _ln_rows(xn, LN_EPS) * lng_ref[...] + lnb_ref[...]


def _post(alpha, y, r, k, v, g, ga, mixb, x, gate, p, tm):
    T, D = x.shape
    W = y.shape[1]
    row = lambda w: pl.BlockSpec((tm, w), lambda i: (i, 0))
    consts = [p["r_k"], p["gn_g"], p["gn_b"], p["hsum"], p["w_ba"], p["w_out"], gate, p["ln1_g"], p["ln1_b"]]
    return pl.pallas_call(
        functools.partial(_post_kernel, alpha),
        grid=(T // tm,),
        in_specs=[row(W)] * 5 + [row(D)] * 3 + [_const_spec(a.shape) for a in consts],
        out_specs=row(D),
        out_shape=jax.ShapeDtypeStruct((T, D), jnp.float32),
        compiler_params=pltpu.CompilerParams(dimension_semantics=("arbitrary",), vmem_limit_bytes=VMEM_LIMIT),
        name="mixer_post",
    )(y, r, k, v, g, ga, mixb, x, *consts)


def _route_kernel(x_ref, sh_ref, sc_ref, rwt_ref, rb_ref, h_ref, wt_ref):
    h = _ln_rows(x_ref[...], LN_EPS) * (1.0 + sc_ref[...]) + sh_ref[...]
    h_ref[...] = h.astype(jnp.bfloat16)
    logits = lax.dot_general(rwt_ref[...], h, _NT, precision=_HI, preferred_element_type=jnp.float32)
    scores = _sigmoid(logits)
    sel = scores + rb_ref[...]
    E = sel.shape[0]
    per = E // N_EXPERT_GROUPS
    rows = [sel[e:e + 1, :] for e in range(E)]
    gscore = []
    for g in range(N_EXPERT_GROUPS):
        m = rows[g * per:(g + 1) * per]
        best = None
        for i in range(per):
            for j in range(i + 1, per):
                s2 = m[i] + m[j]
                best = s2 if best is None else jnp.maximum(best, s2)
        gscore.append(best)
    cur = gscore[0]
    bestg = jnp.zeros(cur.shape, jnp.int32)
    for g in range(1, N_EXPERT_GROUPS):
        take = gscore[g] > cur
        cur = jnp.where(take, gscore[g], cur)
        bestg = jnp.where(take, g, bestg)
    picked = []
    for e in range(E):
        g = e // per
        rank = jnp.zeros(cur.shape, jnp.int32)
        for j in range(g * per, (g + 1) * per):
            if j == e:
                continue
            ahead = (rows[j] > rows[e]) | ((rows[j] == rows[e]) & (j < e))
            rank = rank + ahead.astype(jnp.int32)
        picked.append(jnp.where((bestg == g) & (rank < 2), scores[e:e + 1, :], 0.0))
    total = picked[0]
    for e in range(1, E):
        total = total + picked[e]
    wt_ref[...] = jnp.concatenate(picked, axis=0) / total


def _route(x, shift, scale, router_wt, router_b, tm):
    T, D = x.shape
    E = router_wt.shape[0]
    return pl.pallas_call(
        _route_kernel,
        grid=(T // tm,),
        in_specs=[pl.BlockSpec((tm, D), lambda i: (i, 0)), _const_spec(shift.shape), _const_spec(scale.shape),
                  _const_spec(router_wt.shape), _const_spec(router_b.shape)],
        out_specs=[pl.BlockSpec((tm, D), lambda i: (i, 0)), pl.BlockSpec((E, tm), lambda i: (0, i))],
        out_shape=[jax.ShapeDtypeStruct((T, D), jnp.bfloat16), jax.ShapeDtypeStruct((E, T), jnp.float32)],
        compiler_params=pltpu.CompilerParams(dimension_semantics=("arbitrary",), vmem_limit_bytes=VMEM_LIMIT),
        name="moe_route",
    )(x, shift, scale, router_wt, router_b)


def _moe_kernel(alpha, x_ref, h_ref, wt_ref, eg_ref, eu_ref, ed_ref, gate_ref, lng_ref, lnb_ref, o_ref, acc_ref):
    e = pl.program_id(1)

    @pl.when(e == 0)
    def _():
        acc_ref[...] = jnp.zeros_like(acc_ref)

    h = h_ref[...]
    wt = wt_ref[...]
    lane = lax.broadcasted_iota(jnp.int32, wt.shape, 1)
    wcol = jnp.sum(jnp.where(lane == e, wt, 0.0), axis=-1, keepdims=True)
    gt = jnp.dot(h, eg_ref[0], preferred_element_type=jnp.float32)
    up = jnp.dot(h, eu_ref[0], preferred_element_type=jnp.float32)
    hid = gt * _sigmoid(gt) * up * wcol
    acc_ref[...] += jnp.dot(hid.astype(jnp.bfloat16), ed_ref[0], preferred_element_type=jnp.float32)

    @pl.when(e == pl.num_programs(1) - 1)
    def _():
        xn = alpha * x_ref[...] + (1.0 + gate_ref[...]) * acc_ref[...]
        o_ref[...] = _ln_rows(xn, LN_EPS) * lng_ref[...] + lnb_ref[...]


def _moe(alpha, x, h, wt, eg, eu, ed, gate, ln_g, ln_b, tm):
    T, D = x.shape
    E, _, F = eg.shape
    return pl.pallas_call(
        functools.partial(_moe_kernel, alpha),
        grid=(T // tm, E),
        in_specs=[pl.BlockSpec((tm, D), lambda i, e: (i, 0)),
                  pl.BlockSpec((tm, D), lambda i, e: (i, 0)),
                  pl.BlockSpec((tm, E), lambda i, e: (i, 0)),
                  pl.BlockSpec((1, D, F), lambda i, e: (e, 0, 0)),
                  pl.BlockSpec((1, D, F), lambda i, e: (e, 0, 0)),
                  pl.BlockSpec((1, F, D), lambda i, e: (e, 0, 0)),
                  pl.BlockSpec((1, D), lambda i, e: (0, 0)),
                  pl.BlockSpec((1, D), lambda i, e: (0, 0)),
                  pl.BlockSpec((1, D), lambda i, e: (0, 0))],
        out_specs=pl.BlockSpec((tm, D), lambda i, e: (i, 0)),
        out_shape=jax.ShapeDtypeStruct((T, D), jnp.float32),
        scratch_shapes=[pltpu.VMEM((tm, D), jnp.float32)],
        compiler_params=pltpu.CompilerParams(dimension_semantics=("arbitrary", "arbitrary"),
                                             vmem_limit_bytes=VMEM_LIMIT),
        name="moe_experts",
    )(x, h, wt, eg, eu, ed, gate, ln_g, ln_b)


def kernel(x, c, w_mod, b_mod, w_in, rwkv_mu, w0, w2, a0, a2, g2, k_k, k_a, r_k, gn_g, gn_b, vm0, vm1, vm2,
           sg_ln_g, sg_ln_b, w_s, b_s, w_branch_a, w_branch_b, w_out, ln1_g, ln1_b, router_w, router_b,
           e_gate, e_up, e_down, ln2_g, ln2_b):
    B, T, D = x.shape
    assert B == 1
    L = w_mod.shape[0]
    W = w0.shape[1]
    S = sg_ln_g.shape[1]
    dl, al = w2.shape[1], a2.shape[1]
    assert dl + al == LANES and W % LANES == 0 and S % LANES == 0
    rw_cols = rwkv_mu.shape[1]
    G, CH = w_s.shape[1], w_s.shape[2]
    assert CH == LANES and S // G == HEAD_DIM and W // r_k.shape[1] == HEAD_DIM
    alpha = float((2 * L) ** 0.25)
    bf = jnp.bfloat16
    f32 = jnp.float32

    mod = _modulation(c, w_mod, b_mod)
    head = jnp.arange(W) // HEAD_DIM
    hsum = (head[:, None] == head[None, :]).astype(bf)
    causal = jnp.tril(jnp.ones((CH, CH), bool))
    router_wt = router_w.T
    router_bc = router_b.reshape(-1, 1)

    xt = x.reshape(T, D)
    v_first = None
    for i in range(L):
        sh1, sc1, gt1, sh2, sc2, gt2 = [mod[i, :, j * D:(j + 1) * D] for j in range(6)]
        w_m = jnp.where(causal, w_s[i], 0.0)
        p = dict(
            w_rw=w_in[i, :, :rw_cols].astype(bf),
            w_sg=w_in[i, :, rw_cols:rw_cols + 2 * S].astype(bf),
            w_gt=w_in[i, :, rw_cols + 2 * S:].astype(bf),
            mu=rwkv_mu[i][None], w0=w0[i][None], a0=a0[i][None],
            w2p=jnp.concatenate([w2[i], jnp.zeros((al, W), f32)], 0).astype(bf),
            a2p=jnp.concatenate([jnp.zeros((dl, W), f32), a2[i]], 0).astype(bf),
            g2=g2[i].astype(bf), k_k=k_k[i][None], k_a=k_a[i][None], hsum=hsum,
            sg_ln_g=sg_ln_g[i][None], sg_ln_b=sg_ln_b[i][None],
            wpair=jnp.concatenate([w_m[0::2], w_m[1::2]], axis=2).astype(bf),
            sbias=jnp.repeat(b_s[i].T, HEAD_DIM, axis=1),
            w_bb=w_branch_b[i].astype(bf), w_ba=w_branch_a[i].astype(bf), w_out=w_out[i].astype(bf),
            r_k=r_k[i].reshape(1, W), gn_g=gn_g[i][None], gn_b=gn_b[i][None],
            ln1_g=ln1_g[i][None], ln1_b=ln1_b[i][None],
        )
        if i > 0:
            p.update(vm0=vm0[i - 1][None], vm1=vm1[i - 1].astype(bf), vm2=vm2[i - 1].astype(bf))
        r, k, v, kk, b, lw, g, ga, mixb = _front(xt, sh1, sc1, p, v_first, tm=256)
        if i == 0:
            v_first = v
        y = _scan(r, k, v, kk, b, lw, tt=256)
        xt = _post(alpha, y, r, k, v, g, ga, mixb, xt, gt1, p, tm=256)
        h, wt = _route(xt, sh2, sc2, router_wt, router_bc, tm=512)
        xt = _moe(alpha, xt, h, wt.T, e_gate[i].astype(bf), e_up[i].astype(bf), e_down[i].astype(bf),
                  gt2, ln2_g[i][None], ln2_b[i][None], tm=1024)
    return xt.reshape(B, T, D)
```

```python
import functools

import jax
import jax.numpy as jnp
from jax import lax
from jax.experimental import pallas as pl
from jax.experimental.pallas import tpu as pltpu

LN_EPS = 1e-5
GN_EPS = 64e-5
N_EXPERT_GROUPS = 4
LANES = 128
SUBLANES = 8
MOE_BLOCK = 128
SCAN_CHUNK = 64
HEAD_DIM = 64
MOE_TILE = 1024
VMEM_LIMIT = 56 * 1024 * 1024
MOE_VMEM_LIMIT = 60 * 1024 * 1024

_HI = lax.Precision.HIGHEST
_NT = (((1,), (1,)), ((), ()))
_TN = (((0,), (0,)), ((), ()))


def _bdot(a, b):
    return jnp.dot(a.astype(jnp.bfloat16), b.astype(jnp.bfloat16), preferred_element_type=jnp.float32)


def _split_dot(a, b_exact, terms=2):
    acc = None
    rem = a
    for _ in range(terms):
        piece = rem.astype(jnp.bfloat16)
        part = jnp.dot(piece, b_exact, preferred_element_type=jnp.float32)
        acc = part if acc is None else acc + part
        rem = rem - piece.astype(jnp.float32)
    return acc


def _sigmoid(x):
    return 1.0 / (1.0 + jnp.exp(-x))


def _softplus(x):
    return jnp.maximum(x, 0.0) + jnp.log(1.0 + jnp.exp(-jnp.abs(x)))


def _ln_rows(x, eps):
    mu = jnp.mean(x, axis=-1, keepdims=True)
    xc = x - mu
    var = jnp.mean(xc * xc, axis=-1, keepdims=True)
    return xc * lax.rsqrt(var + eps)


def _const_spec(shape):
    nd = len(shape)
    return pl.BlockSpec(shape, lambda *_: (0,) * nd, pipeline_mode=pl.Buffered(1))


def _mod_kernel(c_ref, w_ref, b_ref, o_ref):
    c = c_ref[...]
    cond = c * _sigmoid(c)
    o_ref[0] = jnp.sum(cond * w_ref[0], axis=0, keepdims=True) + b_ref[0]


def _modulation(c, w_mod, b_mod):
    L, D, D6 = w_mod.shape
    nb = D6 // D
    out = pl.pallas_call(
        _mod_kernel,
        grid=(L, nb),
        in_specs=[pl.BlockSpec((D, 1), lambda l, j: (0, 0)),
                  pl.BlockSpec((1, D, D), lambda l, j: (l, 0, j)),
                  pl.BlockSpec((1, 1, D), lambda l, j: (l, 0, j))],
        out_specs=pl.BlockSpec((1, 1, D), lambda l, j: (l, 0, j)),
        out_shape=jax.ShapeDtypeStruct((L, 1, D6), jnp.float32),
        compiler_params=pltpu.CompilerParams(dimension_semantics=("arbitrary", "arbitrary"),
                                             vmem_limit_bytes=VMEM_LIMIT),
        name="adaln_mod",
    )(c.reshape(D, 1), w_mod, b_mod.reshape(L, 1, D6))
    return out


def _front_kernel(has_vmix, W, *refs):
    (x_ref, sh_ref, sc_ref, wrw_ref, wsg_ref, wgt_ref, mu_ref, w0_ref, w2p_ref, a0_ref, a2p_ref, g2_ref,
     kkw_ref, ka_ref, hsum_ref) = refs[:15]
    pos = 15
    if has_vmix:
        vf_ref, vm0_ref, vm1_ref, vm2_ref = refs[pos:pos + 4]
        pos += 4
    lng_ref, lnb_ref, wpair_ref, sbias_ref, wbb_ref = refs[pos:pos + 5]
    pos += 5
    r_o, k_o, v_o, kk_o, b_o, lw_o, g_o, ga_o, mixb_o = refs[pos:pos + 9]
    carry_ref = refs[pos + 9]

    tm = x_ref.shape[0]

    @pl.when(pl.program_id(0) == 0)
    def _():
        carry_ref[...] = jnp.zeros_like(carry_ref)

    h = _ln_rows(x_ref[...], LN_EPS) * (1.0 + sc_ref[...]) + sh_ref[...]
    hb = h.astype(jnp.bfloat16)

    z = jnp.dot(hb, wrw_ref[...], preferred_element_type=jnp.float32)
    row = lax.broadcasted_iota(jnp.int32, z.shape, 0)
    prev = jnp.where(row == 0, carry_ref[...], pltpu.roll(z, 1, axis=0))
    carry_ref[...] = z[tm - 1:tm, :]
    z = z + (prev - z) * mu_ref[...]
    r = z[:, :W]
    k = z[:, W:2 * W]
    v = z[:, 2 * W:3 * W]
    zwa = z[:, 3 * W:3 * W + LANES]
    zg = z[:, 3 * W + LANES:]
    w_log = -_softplus(-(w0_ref[...] + _bdot(jnp.tanh(zwa), w2p_ref[...]))) - 0.5
    lw_o[...] = -jnp.exp(w_log)
    a = _sigmoid(a0_ref[...] + _bdot(zwa, a2p_ref[...]))
    g_o[...] = _bdot(_sigmoid(zg), g2_ref[...])
    if has_vmix:
        mix = _sigmoid(vm0_ref[...] + _bdot(_bdot(v, vm1_ref[...]), vm2_ref[...]))
        v = v + (vf_ref[...] - v) * mix
    kk = k * kkw_ref[...]
    ss = _split_dot(kk * kk, hsum_ref[...])
    kk = kk / jnp.maximum(jnp.sqrt(ss), 1e-12)
    r_o[...] = r
    k_o[...] = k * (1.0 + (a - 1.0) * ka_ref[...])
    v_o[...] = v
    kk_o[...] = kk
    b_o[...] = kk * a

    zs = jnp.dot(hb, wsg_ref[...], preferred_element_type=jnp.float32)
    zs = zs * (0.5 * (1.0 + jnp.tanh(0.7978845608028654 * (zs + 0.044715 * (zs * zs * zs)))))
    S = zs.shape[1] // 2
    u = zs[:, :S]
    vln = _ln_rows(zs[:, S:], LN_EPS) * lng_ref[...] + lnb_ref[...]
    lo = lax.broadcasted_iota(jnp.int32, (LANES, LANES), 1) < HEAD_DIM
    chunks = []
    for c in range(tm // LANES):
        cols = []
        for p in range(S // LANES):
            vp = vln[c * LANES:(c + 1) * LANES, p * LANES:(p + 1) * LANES]
            vs = jnp.concatenate([jnp.where(lo, vp, 0.0), jnp.where(lo, 0.0, vp)], axis=0)
            cols.append(_bdot(wpair_ref[p], vs))
        chunks.append(jnp.concatenate(cols, axis=1) + sbias_ref[...])
    yb = u * jnp.concatenate(chunks, axis=0)

    zg2 = jnp.dot(hb, wgt_ref[...], preferred_element_type=jnp.float32)
    Dm = zg2.shape[1] // 2
    ga_o[...] = _sigmoid(zg2[:, :Dm])
    mixb_o[...] = _sigmoid(zg2[:, Dm:]) * _bdot(yb, wbb_ref[...])


def _front(x, shift, scale, p, v_first, tm):
    T, D = x.shape
    W = p["hsum"].shape[0]
    has_vmix = v_first is not None
    row = lambda w: pl.BlockSpec((tm, w), lambda i: (i, 0))
    ins = [x, shift, scale, p["w_rw"], p["w_sg"], p["w_gt"], p["mu"], p["w0"], p["w2p"], p["a0"], p["a2p"], p["g2"],
           p["k_k"], p["k_a"], p["hsum"]]
    specs = [row(D)] + [_const_spec(a.shape) for a in ins[1:]]
    if has_vmix:
        extra = [v_first, p["vm0"], p["vm1"], p["vm2"]]
        ins += extra
        specs += [row(W)] + [_const_spec(a.shape) for a in extra[1:]]
    tail = [p["sg_ln_g"], p["sg_ln_b"], p["wpair"], p["sbias"], p["w_bb"]]
    ins += tail
    specs += [_const_spec(a.shape) for a in tail]
    f32 = jnp.float32
    out_shape = [jax.ShapeDtypeStruct((T, W), f32)] * 7 + [jax.ShapeDtypeStruct((T, D), f32)] * 2
    out_specs = [row(W)] * 7 + [row(D)] * 2
    return pl.pallas_call(
        functools.partial(_front_kernel, has_vmix, W),
        grid=(T // tm,),
        in_specs=specs,
        out_specs=out_specs,
        out_shape=out_shape,
        scratch_shapes=[pltpu.VMEM((1, p["w_rw"].shape[1]), f32)],
        compiler_params=pltpu.CompilerParams(dimension_semantics=("arbitrary",), vmem_limit_bytes=VMEM_LIMIT),
        name="mixer_front",
    )(*ins)


def _scan_kernel(r_ref, k_ref, v_ref, kk_ref, b_ref, lw_ref, y_ref, s_ref):
    C = SCAN_CHUNK
    P = LANES
    bf = jnp.bfloat16
    tt, width = lw_ref.shape
    n_chunks = tt // C
    n_blocks = width // P

    @pl.when(pl.program_id(0) == 0)
    def _():
        s_ref[...] = jnp.zeros_like(s_ref)

    lw = lw_ref[...]
    ti = lax.broadcasted_iota(jnp.int32, (tt, tt), 0)
    tj = lax.broadcasted_iota(jnp.int32, (tt, tt), 1)
    same_chunk = (ti // C) == (tj // C)
    pre = jnp.concatenate([jnp.where(same_chunk & (ti >= tj), 1.0, 0.0),
                           jnp.where(same_chunk, 1.0, 0.0)], axis=0).astype(bf)
    sums = None
    rem = lw
    for _ in range(3):
        piece = rem.astype(bf)
        part = jnp.dot(pre, piece, preferred_element_type=jnp.float32)
        sums = part if sums is None else sums + part
        rem = rem - piece.astype(jnp.float32)
    cum, tot = sums[:tt], sums[tt:]
    w_inv = jnp.exp(-cum)
    w_end = jnp.exp(tot - cum)
    w_tot = jnp.exp(tot)

    kk = kk_ref[...]
    bb = b_ref[...]
    kx = k_ref[...]
    a_t = -kk * jnp.exp(cum - lw)
    b_t = (bb * w_inv).astype(bf)
    k_t = (kx * w_inv).astype(bf)
    r_t = r_ref[...] * jnp.exp(cum)
    b_h = (bb * w_end).astype(bf)
    k_h = (kx * w_end).astype(bf)
    vv = v_ref[...]

    ri = lax.broadcasted_iota(jnp.int32, (P, P), 0)
    ci = lax.broadcasted_iota(jnp.int32, (P, P), 1)
    same = (ri >= C) == (ci >= C)
    strict = same & (ri > ci)
    incl = same & (ri >= ci)
    eye = ri == ci
    lo = lax.broadcasted_iota(jnp.int32, (C, P), 1) < HEAD_DIM

    def stack(m):
        return jnp.concatenate([jnp.where(lo, m, 0.0), jnp.where(lo, 0.0, m)], axis=0)

    def fold(m):
        return m[:C] + m[C:]

    def mm(a, b, dims=(((1,), (0,)), ((), ()))):
        return lax.dot_general(a, b, dims, preferred_element_type=jnp.float32)

    chains = [(j, p) for j in range(n_chunks) for p in range(n_blocks)]
    n = len(chains)
    cut = lambda m, j, p: m[j * C:(j + 1) * C, p * P:(p + 1) * P]
    a_s = [stack(cut(a_t, j, p)).astype(bf) for j, p in chains]
    r_s = [stack(cut(r_t, j, p)) for j, p in chains]
    v_s = [stack(cut(vv, j, p)).astype(bf) for j, p in chains]
    q = [mm(jnp.concatenate([a_s[i], r_s[i].astype(bf)], axis=0),
            jnp.concatenate([cut(b_t, j, p)] * 2 + [cut(k_t, j, p)] * 2, axis=0), _NT)
         for i, (j, p) in enumerate(chains)]
    l_ab = [jnp.where(strict, m[:P, :P], 0.0) for m in q]
    low = [jnp.concatenate([jnp.where(strict, m[:P, P:], 0.0), jnp.where(incl, m[P:, P:], 0.0)], axis=0).astype(bf)
           for m in q]
    p_rb = [jnp.where(incl, m[P:, :P], 0.0).astype(bf) for m in q]
    inv = [jnp.where(eye, 1.0, jnp.where((ri >> 1) == (ci >> 1), m, 0.0)) for m in l_ab]
    for lg in range(1, 6):
        off = ((ri >> (lg + 1)) == (ci >> (lg + 1))) & (((ri >> lg) & 1) == 1) & (((ci >> lg) & 1) == 0)
        inv_b = [m.astype(bf) for m in inv]
        tmp = [mm(jnp.where(off, l_ab[i], 0.0).astype(bf), inv_b[i]) for i in range(n)]
        inv = [inv[i] + mm(inv_b[i], tmp[i].astype(bf)) for i in range(n)]
    lv = [mm(low[i], v_s[i]) for i in range(n)]
    ta = [mm(inv[i].astype(bf), jnp.concatenate([a_s[i], lv[i][:P].astype(bf)], axis=1))
          for i in range(n)]
    pa = [mm(p_rb[i], ta[i].astype(bf)) + jnp.concatenate([r_s[i], lv[i][P:]], axis=1) for i in range(n)]
    a_p = [fold(m[:, :P]) for m in ta]
    u_v = [fold(m[:, P:]) for m in ta]
    r_p = [fold(m[:, :P]).astype(bf) for m in pa]
    y_loc = [fold(m[:, P:]) for m in pa]
    zero = jnp.zeros((C, P), jnp.float32)
    gh = [mm(jnp.concatenate([cut(b_h, j, p), cut(k_h, j, p)], axis=0),
             jnp.concatenate([jnp.concatenate([a_p[i], u_v[i]], axis=1),
                              jnp.concatenate([zero, cut(vv, j, p)], axis=1)], axis=0).astype(bf), _TN)
          for i, (j, p) in enumerate(chains)]
    g_m = [(jnp.where(same, gh[i][:, :P], 0.0) + jnp.where(eye, cut(w_tot, j, p)[:1], 0.0)).astype(bf)
           for i, (j, p) in enumerate(chains)]
    h_m = [jnp.where(same, m[:, P:], 0.0) for m in gh]
    state = [s_ref[p] for p in range(n_blocks)]
    for i, (j, p) in enumerate(chains):
        s_b = state[p].astype(bf)
        y_ref[j * C:(j + 1) * C, p * P:(p + 1) * P] = mm(r_p[i], s_b) + y_loc[i]
        state[p] = mm(g_m[i], s_b) + h_m[i]
    for p in range(n_blocks):
        s_ref[p] = state[p]


def _scan(r, k, v, kk, b, lw, tt):
    T, W = r.shape
    spec = pl.BlockSpec((tt, W), lambda i: (i, 0))
    return pl.pallas_call(
        _scan_kernel,
        grid=(T // tt,),
        in_specs=[spec] * 6,
        out_specs=spec,
        out_shape=jax.ShapeDtypeStruct((T, W), jnp.float32),
        scratch_shapes=[pltpu.VMEM((W // LANES, LANES, LANES), jnp.float32)],
        compiler_params=pltpu.CompilerParams(dimension_semantics=("arbitrary",), vmem_limit_bytes=VMEM_LIMIT),
        name="wkv7_scan",
    )(r, k, v, kk, b, lw)


def _post_kernel(alpha, y_ref, r_ref, k_ref, v_ref, g_ref, ga_ref, mixb_ref, x_ref, rk_ref, gng_ref, gnb_ref,
                 hsum_ref, wba_ref, wout_ref, gate_ref, lng_ref, lnb_ref, o_ref):
    y = y_ref[...]
    hs = hsum_ref[...]
    inv_n = 1.0 / HEAD_DIM
    mu = _split_dot(y, hs) * inv_n
    yc = y - mu
    var = _split_dot(yc * yc, hs) * inv_n
    yn = yc * lax.rsqrt(var + GN_EPS) * gng_ref[...] + gnb_ref[...]
    v = v_ref[...]
    bonus = _split_dot(r_ref[...] * k_ref[...] * rk_ref[...], hs)
    ya = (yn + bonus * v) * g_ref[...]
    mixed = ga_ref[...] * _bdot(ya, wba_ref[...]) + mixb_ref[...]
    mixed = _bdot(mixed, wout_ref[...])
    xn = alpha * x_ref[...] + (1.0 + gate_ref[...]) * mixed
    o_ref[...] = _ln_rows(xn, LN_EPS) * lng_ref[...] + lnb_ref[...]


def _post(alpha, y, r, k, v, g, ga, mixb, x, gate, p, tm):
    T, D = x.shape
    W = y.shape[1]
    row = lambda w: pl.BlockSpec((tm, w), lambda i: (i, 0))
    consts = [p["r_k"], p["gn_g"], p["gn_b"], p["hsum"], p["w_ba"], p["w_out"], gate, p["ln1_g"], p["ln1_b"]]
    return pl.pallas_call(
        functools.partial(_post_kernel, alpha),
        grid=(T // tm,),
        in_specs=[row(W)] * 5 + [row(D)] * 3 + [_const_spec(a.shape) for a in consts],
        out_specs=row(D),
        out_shape=jax.ShapeDtypeStruct((T, D), jnp.float32),
        compiler_params=pltpu.CompilerParams(dimension_semantics=("arbitrary",), vmem_limit_bytes=VMEM_LIMIT),
        name="mixer_post",
    )(y, r, k, v, g, ga, mixb, x, *consts)


def _route_kernel(x_ref, sh_ref, sc_ref, rwt_ref, rb_ref, su_ref, h_ref, wt_ref, gsel_ref, pos_ref, cnt_ref):
    h = _ln_rows(x_ref[...], LN_EPS) * (1.0 + sc_ref[...]) + sh_ref[...]
    h_ref[...] = h.astype(jnp.bfloat16)
    logits = lax.dot_general(rwt_ref[...], h, _NT, precision=_HI, preferred_element_type=jnp.float32)
    scores = _sigmoid(logits)
    sel = scores + rb_ref[...]
    E = sel.shape[0]
    per = E // N_EXPERT_GROUPS
    rows = [sel[e:e + 1, :] for e in range(E)]
    gscore = []
    for g in range(N_EXPERT_GROUPS):
        m = rows[g * per:(g + 1) * per]
        best = None
        for i in range(per):
            for j in range(i + 1, per):
                s2 = m[i] + m[j]
                best = s2 if best is None else jnp.maximum(best, s2)
        gscore.append(best)
    cur = gscore[0]
    bestg = jnp.zeros(cur.shape, jnp.int32)
    for g in range(1, N_EXPERT_GROUPS):
        take = gscore[g] > cur
        cur = jnp.where(take, gscore[g], cur)
        bestg = jnp.where(take, g, bestg)
    picked = []
    for e in range(E):
        g = e // per
        rank = jnp.zeros(cur.shape, jnp.int32)
        for j in range(g * per, (g + 1) * per):
            if j == e:
                continue
            ahead = (rows[j] > rows[e]) | ((rows[j] == rows[e]) & (j < e))
            rank = rank + ahead.astype(jnp.int32)
        picked.append(jnp.where((bestg == g) & (rank < 2), scores[e:e + 1, :], 0.0))
    total = picked[0]
    for e in range(1, E):
        total = total + picked[e]
    wt_ref[...] = jnp.concatenate(picked, axis=0) / total
    gsel = jnp.concatenate([jnp.where(bestg == g, 1.0, 0.0) for g in range(N_EXPERT_GROUPS)]
                           + [jnp.zeros_like(cur)] * (2 * SUBLANES - N_EXPERT_GROUPS), axis=0)
    gsel_ref[...] = gsel[:SUBLANES]
    pos_ref[...] = jnp.dot(gsel.astype(jnp.bfloat16), su_ref[...], preferred_element_type=jnp.float32)[:SUBLANES]
    cnt_ref[0] = jnp.broadcast_to(jnp.sum(gsel[:SUBLANES], axis=-1, keepdims=True), (SUBLANES, LANES))


def _route(x, shift, scale, router_wt, router_b, su, tm):
    T, D = x.shape
    E = router_wt.shape[0]
    f32 = jnp.float32
    return pl.pallas_call(
        _route_kernel,
        grid=(T // tm,),
        in_specs=[pl.BlockSpec((tm, D), lambda i: (i, 0)), _const_spec(shift.shape), _const_spec(scale.shape),
                  _const_spec(router_wt.shape), _const_spec(router_b.shape), _const_spec(su.shape)],
        out_specs=[pl.BlockSpec((tm, D), lambda i: (i, 0)), pl.BlockSpec((E, tm), lambda i: (0, i)),
                   pl.BlockSpec((SUBLANES, tm), lambda i: (0, i)), pl.BlockSpec((SUBLANES, tm), lambda i: (0, i)),
                   pl.BlockSpec((1, SUBLANES, LANES), lambda i: (i, 0, 0))],
        out_shape=[jax.ShapeDtypeStruct((T, D), jnp.bfloat16), jax.ShapeDtypeStruct((E, T), f32),
                   jax.ShapeDtypeStruct((SUBLANES, T), f32), jax.ShapeDtypeStruct((SUBLANES, T), f32),
                   jax.ShapeDtypeStruct((T // tm, SUBLANES, LANES), f32)],
        compiler_params=pltpu.CompilerParams(dimension_semantics=("arbitrary",), vmem_limit_bytes=VMEM_LIMIT),
        name="moe_route",
    )(x, shift, scale, router_wt, router_b, su)


def _moe_kernel(alpha, bg_ref, nblk_ref, off_ref, x_ref, h_ref, wt_ref, gsel_ref, pos_ref, wgu_ref, wd_ref,
                gate_ref, lng_ref, lnb_ref, o_ref, xs_ref, ws_ref, ob_ref, at_ref):
    i = pl.program_id(0)
    b = pl.program_id(1)
    nb = pl.num_programs(1)
    tm = x_ref.shape[0]
    rmax = xs_ref.shape[0]
    E = wt_ref.shape[1]
    G = N_EXPERT_GROUPS
    per = E // G
    F = wd_ref.shape[1] // per
    bf, f32 = jnp.bfloat16, jnp.float32
    dot = functools.partial(jnp.dot, preferred_element_type=f32)

    @pl.when(b == 0)
    def _():
        rowi = lax.broadcasted_iota(jnp.int32, (SUBLANES, 1), 0)
        offs = jnp.zeros((SUBLANES, 1), f32)
        for g in range(G):
            offs = jnp.where(rowi == g, off_ref[i * G + g].astype(f32), offs)
        slot = jnp.sum(gsel_ref[...] * (pos_ref[...] + offs), axis=0, keepdims=True)
        at_ref[...] = jnp.transpose(jnp.broadcast_to(slot, (LANES, tm)))
        h = h_ref[...]
        wt = wt_ref[...]
        w_hi = wt.astype(bf)
        w_r = wt - w_hi.astype(f32)
        w_mid = w_r.astype(bf)
        w_lo = (w_r - w_mid.astype(f32)).astype(bf)
        ch = 2 * MOE_BLOCK
        for c in range(rmax // ch):
            srow = (lax.broadcasted_iota(jnp.int32, (ch, tm), 0) + c * ch).astype(f32)
            sel = jnp.where(srow == slot, 1.0, 0.0).astype(bf)
            xs_ref[c * ch:(c + 1) * ch, :] = dot(sel, h).astype(bf)
            ws_ref[c * ch:(c + 1) * ch, :] = dot(sel, w_hi) + dot(sel, w_mid) + dot(sel, w_lo)

    rows = pl.ds(pl.multiple_of(b * MOE_BLOCK, MOE_BLOCK), MOE_BLOCK)

    @pl.when(b < nblk_ref[i])
    def _():
        g = bg_ref[i * nb + b]
        a = dot(xs_ref[rows, :], wgu_ref[0])
        wsb = ws_ref[rows, :]
        lane = lax.broadcasted_iota(jnp.int32, wsb.shape, 1)
        wfull = jnp.concatenate(
            [jnp.broadcast_to(jnp.sum(jnp.where(lane == g * per + e, wsb, 0.0), axis=-1, keepdims=True),
                              (MOE_BLOCK, F)) for e in range(per)], axis=1)
        gt, up = a[:, :per * F], a[:, per * F:]
        hid = gt * _sigmoid(gt) * up * wfull
        ob_ref[rows, :] = dot(hid.astype(bf), wd_ref[0]).astype(bf)

    @pl.when(b >= nblk_ref[i])
    def _():
        ob_ref[rows, :] = jnp.zeros((MOE_BLOCK, ob_ref.shape[1]), bf)

    @pl.when(b == nb - 1)
    def _():
        slot_c = at_ref[...]
        lane = lax.broadcasted_iota(jnp.int32, slot_c.shape, 1).astype(f32)
        back = jnp.concatenate([jnp.where(slot_c == lane + float(j * LANES), 1.0, 0.0).astype(bf)
                                for j in range(rmax // LANES)], axis=1)
        xn = alpha * x_ref[...] + (1.0 + gate_ref[...]) * dot(back, ob_ref[...])
        o_ref[...] = _ln_rows(xn, LN_EPS) * lng_ref[...] + lnb_ref[...]


def _moe(alpha, x, h, wt, gsel, pos, cnt, wgu, wd, gate, ln_g, ln_b, tm):
    T, D = x.shape
    E = wt.shape[1]
    G = N_EXPERT_GROUPS
    nt = T // tm
    nb = tm // MOE_BLOCK + G
    rmax = nb * MOE_BLOCK
    cnt_i = cnt[:, :G, 0].astype(jnp.int32)
    padded = (cnt_i + MOE_BLOCK - 1) // MOE_BLOCK * MOE_BLOCK
    ends = jnp.cumsum(padded, axis=1)
    off = (ends - padded).reshape(-1)
    nblk = ends[:, -1] // MOE_BLOCK
    blk = jnp.minimum(jnp.arange(nb, dtype=jnp.int32)[None, :], nblk[:, None] - 1)
    bg = jnp.sum((blk[:, :, None] * MOE_BLOCK >= ends[:, None, :]).astype(jnp.int32), axis=-1).reshape(-1)
    tok = lambda w: pl.BlockSpec((tm, w), lambda i, b, *_: (i, 0))
    tokt = pl.BlockSpec((SUBLANES, tm), lambda i, b, *_: (0, i))
    vec = pl.BlockSpec((1, D), lambda i, b, *_: (0, 0))
    return pl.pallas_call(
        functools.partial(_moe_kernel, alpha),
        grid_spec=pltpu.PrefetchScalarGridSpec(
            num_scalar_prefetch=3,
            grid=(nt, nb),
            in_specs=[tok(D), tok(D), tok(E), tokt, tokt,
                      pl.BlockSpec((1,) + wgu.shape[1:], lambda i, b, bg, nblk, off: (bg[i * nb + b], 0, 0)),
                      pl.BlockSpec((1,) + wd.shape[1:], lambda i, b, bg, nblk, off: (bg[i * nb + b], 0, 0)),
                      vec, vec, vec],
            out_specs=tok(D),
            scratch_shapes=[pltpu.VMEM((rmax, D), jnp.bfloat16), pltpu.VMEM((rmax, E), jnp.float32),
                            pltpu.VMEM((rmax, D), jnp.bfloat16), pltpu.VMEM((tm, LANES), jnp.float32)]),
        out_shape=jax.ShapeDtypeStruct((T, D), jnp.float32),
        compiler_params=pltpu.CompilerParams(dimension_semantics=("arbitrary", "arbitrary"),
                                             vmem_limit_bytes=MOE_VMEM_LIMIT),
        name="moe_experts",
    )(bg, nblk.astype(jnp.int32), off.astype(jnp.int32), x, h, wt, gsel, pos, wgu, wd, gate, ln_g, ln_b)


def kernel(x, c, w_mod, b_mod, w_in, rwkv_mu, w0, w2, a0, a2, g2, k_k, k_a, r_k, gn_g, gn_b, vm0, vm1, vm2,
           sg_ln_g, sg_ln_b, w_s, b_s, w_branch_a, w_branch_b, w_out, ln1_g, ln1_b, router_w, router_b,
           e_gate, e_up, e_down, ln2_g, ln2_b):
    B, T, D = x.shape
    assert B == 1
    L = w_mod.shape[0]
    W = w0.shape[1]
    S = sg_ln_g.shape[1]
    dl, al = w2.shape[1], a2.shape[1]
    assert dl + al == LANES and W % LANES == 0 and S % LANES == 0
    rw_cols = rwkv_mu.shape[1]
    G, CH = w_s.shape[1], w_s.shape[2]
    assert CH == LANES and S // G == HEAD_DIM and W // r_k.shape[1] == HEAD_DIM
    alpha = float((2 * L) ** 0.25)
    bf = jnp.bfloat16
    f32 = jnp.float32

    mod = _modulation(c, w_mod, b_mod)
    head = jnp.arange(W) // HEAD_DIM
    hsum = (head[:, None] == head[None, :]).astype(bf)
    causal = jnp.tril(jnp.ones((CH, CH), bool))
    router_wt = router_w.T
    router_bc = router_b.reshape(-1, 1)
    E, _, F = e_gate.shape[1:]
    NG = N_EXPERT_GROUPS
    per = E // NG
    tidx = jnp.arange(MOE_TILE)
    su = (tidx[:, None] < tidx[None, :]).astype(bf)

    xt = x.reshape(T, D)
    v_first = None
    for i in range(L):
        sh1, sc1, gt1, sh2, sc2, gt2 = [mod[i, :, j * D:(j + 1) * D] for j in range(6)]
        w_m = jnp.where(causal, w_s[i], 0.0)
        p = dict(
            w_rw=w_in[i, :, :rw_cols].astype(bf),
            w_sg=w_in[i, :, rw_cols:rw_cols + 2 * S].astype(bf),
            w_gt=w_in[i, :, rw_cols + 2 * S:].astype(bf),
            mu=rwkv_mu[i][None], w0=w0[i][None], a0=a0[i][None],
            w2p=jnp.concatenate([w2[i], jnp.zeros((al, W), f32)], 0).astype(bf),
            a2p=jnp.concatenate([jnp.zeros((dl, W), f32), a2[i]], 0).astype(bf),
            g2=g2[i].astype(bf), k_k=k_k[i][None], k_a=k_a[i][None], hsum=hsum,
            sg_ln_g=sg_ln_g[i][None], sg_ln_b=sg_ln_b[i][None],
            wpair=jnp.concatenate([w_m[0::2], w_m[1::2]], axis=2).astype(bf),
            sbias=jnp.repeat(b_s[i].T, HEAD_DIM, axis=1),
            w_bb=w_branch_b[i].astype(bf), w_ba=w_branch_a[i].astype(bf), w_out=w_out[i].astype(bf),
            r_k=r_k[i].reshape(1, W), gn_g=gn_g[i][None], gn_b=gn_b[i][None],
            ln1_g=ln1_g[i][None], ln1_b=ln1_b[i][None],
        )
        if i > 0:
            p.update(vm0=vm0[i - 1][None], vm1=vm1[i - 1].astype(bf), vm2=vm2[i - 1].astype(bf))
        r, k, v, kk, b, lw, g, ga, mixb = _front(xt, sh1, sc1, p, v_first, tm=256)
        if i == 0:
            v_first = v
        y = _scan(r, k, v, kk, b, lw, tt=256)
        xt = _post(alpha, y, r, k, v, g, ga, mixb, xt, gt1, p, tm=256)
        h, wt, gsel, pos, cnt = _route(xt, sh2, sc2, router_wt, router_bc, su, tm=MOE_TILE)
        wgu = jnp.concatenate([e_gate[i].reshape(NG, per, D, F).transpose(0, 2, 1, 3).reshape(NG, D, per * F),
                               e_up[i].reshape(NG, per, D, F).transpose(0, 2, 1, 3).reshape(NG, D, per * F)],
                              axis=2).astype(bf)
        wd = e_down[i].reshape(NG, per * F, D).astype(bf)
        xt = _moe(alpha, xt, h, wt.T, gsel, pos, cnt, wgu, wd, gt2, ln2_g[i][None], ln2_b[i][None], tm=MOE_TILE)
    return xt.reshape(B, T, D)
```

```python
import functools

import jax
import jax.numpy as jnp
from jax import lax
from jax.experimental import pallas as pl
from jax.experimental.pallas import tpu as pltpu

LN_EPS = 1e-5
GN_EPS = 64e-5
N_EXPERT_GROUPS = 4
LANES = 128
SUBLANES = 8
MOE_BLOCK = 128
SCAN_CHUNK = 64
HEAD_DIM = 64
MOE_TILE = 1024
VMEM_LIMIT = 56 * 1024 * 1024

_HI = lax.Precision.HIGHEST
_NT = (((1,), (1,)), ((), ()))
_TN = (((0,), (0,)), ((), ()))


def _bdot(a, b):
    return jnp.dot(a.astype(jnp.bfloat16), b.astype(jnp.bfloat16), preferred_element_type=jnp.float32)


def _split_dot(a, b_exact, terms=2):
    acc = None
    rem = a
    for _ in range(terms):
        piece = rem.astype(jnp.bfloat16)
        part = jnp.dot(piece, b_exact, preferred_element_type=jnp.float32)
        acc = part if acc is None else acc + part
        rem = rem - piece.astype(jnp.float32)
    return acc


def _sigmoid(x):
    return 1.0 / (1.0 + jnp.exp(-x))


def _softplus(x):
    return jnp.maximum(x, 0.0) + jnp.log(1.0 + jnp.exp(-jnp.abs(x)))


def _ln_rows(x, eps):
    mu = jnp.mean(x, axis=-1, keepdims=True)
    xc = x - mu
    var = jnp.mean(xc * xc, axis=-1, keepdims=True)
    return xc * lax.rsqrt(var + eps)


def _const_spec(shape):
    nd = len(shape)
    return pl.BlockSpec(shape, lambda *_: (0,) * nd, pipeline_mode=pl.Buffered(1))


def _mod_kernel(c_ref, w_ref, b_ref, o_ref):
    c = c_ref[...]
    cond = c * _sigmoid(c)
    o_ref[0] = jnp.sum(cond * w_ref[0], axis=0, keepdims=True) + b_ref[0]


def _modulation(c, w_mod, b_mod):
    L, D, D6 = w_mod.shape
    nb = D6 // D
    out = pl.pallas_call(
        _mod_kernel,
        grid=(L, nb),
        in_specs=[pl.BlockSpec((D, 1), lambda l, j: (0, 0)),
                  pl.BlockSpec((1, D, D), lambda l, j: (l, 0, j)),
                  pl.BlockSpec((1, 1, D), lambda l, j: (l, 0, j))],
        out_specs=pl.BlockSpec((1, 1, D), lambda l, j: (l, 0, j)),
        out_shape=jax.ShapeDtypeStruct((L, 1, D6), jnp.float32),
        compiler_params=pltpu.CompilerParams(dimension_semantics=("arbitrary", "arbitrary"),
                                             vmem_limit_bytes=VMEM_LIMIT),
        name="adaln_mod",
    )(c.reshape(D, 1), w_mod, b_mod.reshape(L, 1, D6))
    return out


def _front_kernel(has_vmix, W, *refs):
    (x_ref, sh_ref, sc_ref, wrw_ref, wsg_ref, wgt_ref, mu_ref, w0_ref, w2p_ref, a0_ref, a2p_ref, g2_ref,
     kkw_ref, ka_ref, hsum_ref) = refs[:15]
    pos = 15
    if has_vmix:
        vf_ref, vm0_ref, vm1_ref, vm2_ref = refs[pos:pos + 4]
        pos += 4
    lng_ref, lnb_ref, wpair_ref, sbias_ref, wbb_ref = refs[pos:pos + 5]
    pos += 5
    r_o, k_o, v_o, kk_o, b_o, lw_o, g_o, ga_o, mixb_o = refs[pos:pos + 9]
    carry_ref = refs[pos + 9]

    tm = x_ref.shape[0]

    @pl.when(pl.program_id(0) == 0)
    def _():
        carry_ref[...] = jnp.zeros_like(carry_ref)

    h = _ln_rows(x_ref[...], LN_EPS) * (1.0 + sc_ref[...]) + sh_ref[...]
    hb = h.astype(jnp.bfloat16)

    z = jnp.dot(hb, wrw_ref[...], preferred_element_type=jnp.float32)
    row = lax.broadcasted_iota(jnp.int32, z.shape, 0)
    prev = jnp.where(row == 0, carry_ref[...], pltpu.roll(z, 1, axis=0))
    carry_ref[...] = z[tm - 1:tm, :]
    z = z + (prev - z) * mu_ref[...]
    r = z[:, :W]
    k = z[:, W:2 * W]
    v = z[:, 2 * W:3 * W]
    zwa = z[:, 3 * W:3 * W + LANES]
    zg = z[:, 3 * W + LANES:]
    w_log = -_softplus(-(w0_ref[...] + _bdot(jnp.tanh(zwa), w2p_ref[...]))) - 0.5
    lw_o[...] = -jnp.exp(w_log)
    a = _sigmoid(a0_ref[...] + _bdot(zwa, a2p_ref[...]))
    g_o[...] = _bdot(_sigmoid(zg), g2_ref[...])
    if has_vmix:
        mix = _sigmoid(vm0_ref[...] + _bdot(_bdot(v, vm1_ref[...]), vm2_ref[...]))
        v = v + (vf_ref[...] - v) * mix
    kk = k * kkw_ref[...]
    ss = _split_dot(kk * kk, hsum_ref[...])
    kk = kk / jnp.maximum(jnp.sqrt(ss), 1e-12)
    r_o[...] = r
    k_o[...] = k * (1.0 + (a - 1.0) * ka_ref[...])
    v_o[...] = v
    kk_o[...] = kk
    b_o[...] = kk * a

    zs = jnp.dot(hb, wsg_ref[...], preferred_element_type=jnp.float32)
    zs = zs * (0.5 * (1.0 + jnp.tanh(0.7978845608028654 * (zs + 0.044715 * (zs * zs * zs)))))
    S = zs.shape[1] // 2
    u = zs[:, :S]
    vln = _ln_rows(zs[:, S:], LN_EPS) * lng_ref[...] + lnb_ref[...]
    lo = lax.broadcasted_iota(jnp.int32, (LANES, LANES), 1) < HEAD_DIM
    chunks = []
    for c in range(tm // LANES):
        cols = []
        for p in range(S // LANES):
            vp = vln[c * LANES:(c + 1) * LANES, p * LANES:(p + 1) * LANES]
            vs = jnp.concatenate([jnp.where(lo, vp, 0.0), jnp.where(lo, 0.0, vp)], axis=0)
            cols.append(_bdot(wpair_ref[p], vs))
        chunks.append(jnp.concatenate(cols, axis=1) + sbias_ref[...])
    yb = u * jnp.concatenate(chunks, axis=0)

    zg2 = jnp.dot(hb, wgt_ref[...], preferred_element_type=jnp.float32)
    Dm = zg2.shape[1] // 2
    ga_o[...] = _sigmoid(zg2[:, :Dm])
    mixb_o[...] = _sigmoid(zg2[:, Dm:]) * _bdot(yb, wbb_ref[...])


def _front(x, shift, scale, p, v_first, tm):
    T, D = x.shape
    W = p["hsum"].shape[0]
    has_vmix = v_first is not None
    row = lambda w: pl.BlockSpec((tm, w), lambda i: (i, 0))
    ins = [x, shift, scale, p["w_rw"], p["w_sg"], p["w_gt"], p["mu"], p["w0"], p["w2p"], p["a0"], p["a2p"], p["g2"],
           p["k_k"], p["k_a"], p["hsum"]]
    specs = [row(D)] + [_const_spec(a.shape) for a in ins[1:]]
    if has_vmix:
        extra = [v_first, p["vm0"], p["vm1"], p["vm2"]]
        ins += extra
        specs += [row(W)] + [_const_spec(a.shape) for a in extra[1:]]
    tail = [p["sg_ln_g"], p["sg_ln_b"], p["wpair"], p["sbias"], p["w_bb"]]
    ins += tail
    specs += [_const_spec(a.shape) for a in tail]
    f32 = jnp.float32
    out_shape = [jax.ShapeDtypeStruct((T, W), f32)] * 7 + [jax.ShapeDtypeStruct((T, D), f32)] * 2
    out_specs = [row(W)] * 7 + [row(D)] * 2
    return pl.pallas_call(
        functools.partial(_front_kernel, has_vmix, W),
        grid=(T // tm,),
        in_specs=specs,
        out_specs=out_specs,
        out_shape=out_shape,
        scratch_shapes=[pltpu.VMEM((1, p["w_rw"].shape[1]), f32)],
        compiler_params=pltpu.CompilerParams(dimension_semantics=("arbitrary",), vmem_limit_bytes=VMEM_LIMIT),
        name="mixer_front",
    )(*ins)


def _scan_kernel(r_ref, k_ref, v_ref, kk_ref, b_ref, lw_ref, y_ref, s_ref):
    C = SCAN_CHUNK
    P = LANES
    bf = jnp.bfloat16
    tt, width = lw_ref.shape
    n_chunks = tt // C
    n_blocks = width // P

    @pl.when(pl.program_id(0) == 0)
    def _():
        s_ref[...] = jnp.zeros_like(s_ref)

    lw = lw_ref[...]
    ti = lax.broadcasted_iota(jnp.int32, (tt, tt), 0)
    tj = lax.broadcasted_iota(jnp.int32, (tt, tt), 1)
    same_chunk = (ti // C) == (tj // C)
    pre = jnp.concatenate([jnp.where(same_chunk & (ti >= tj), 1.0, 0.0),
                           jnp.where(same_chunk, 1.0, 0.0)], axis=0).astype(bf)
    sums = None
    rem = lw
    for _ in range(3):
        piece = rem.astype(bf)
        part = jnp.dot(pre, piece, preferred_element_type=jnp.float32)
        sums = part if sums is None else sums + part
        rem = rem - piece.astype(jnp.float32)
    cum, tot = sums[:tt], sums[tt:]
    w_inv = jnp.exp(-cum)
    w_end = jnp.exp(tot - cum)
    w_tot = jnp.exp(tot)

    kk = kk_ref[...]
    bb = b_ref[...]
    kx = k_ref[...]
    a_t = -kk * jnp.exp(cum - lw)
    b_t = (bb * w_inv).astype(bf)
    k_t = (kx * w_inv).astype(bf)
    r_t = r_ref[...] * jnp.exp(cum)
    b_h = (bb * w_end).astype(bf)
    k_h = (kx * w_end).astype(bf)
    vv = v_ref[...]

    ri = lax.broadcasted_iota(jnp.int32, (P, P), 0)
    ci = lax.broadcasted_iota(jnp.int32, (P, P), 1)
    same = (ri >= C) == (ci >= C)
    strict = same & (ri > ci)
    incl = same & (ri >= ci)
    eye = ri == ci
    lo = lax.broadcasted_iota(jnp.int32, (C, P), 1) < HEAD_DIM

    def stack(m):
        return jnp.concatenate([jnp.where(lo, m, 0.0), jnp.where(lo, 0.0, m)], axis=0)

    def fold(m):
        return m[:C] + m[C:]

    def mm(a, b, dims=(((1,), (0,)), ((), ()))):
        return lax.dot_general(a, b, dims, preferred_element_type=jnp.float32)

    chains = [(j, p) for j in range(n_chunks) for p in range(n_blocks)]
    n = len(chains)
    cut = lambda m, j, p: m[j * C:(j + 1) * C, p * P:(p + 1) * P]
    a_s = [stack(cut(a_t, j, p)).astype(bf) for j, p in chains]
    r_s = [stack(cut(r_t, j, p)) for j, p in chains]
    v_s = [stack(cut(vv, j, p)).astype(bf) for j, p in chains]
    q = [mm(jnp.concatenate([a_s[i], r_s[i].astype(bf)], axis=0),
            jnp.concatenate([cut(b_t, j, p)] * 2 + [cut(k_t, j, p)] * 2, axis=0), _NT)
         for i, (j, p) in enumerate(chains)]
    l_ab = [jnp.where(strict, m[:P, :P], 0.0) for m in q]
    low = [jnp.concatenate([jnp.where(strict, m[:P, P:], 0.0), jnp.where(incl, m[P:, P:], 0.0)], axis=0).astype(bf)
           for m in q]
    p_rb = [jnp.where(incl, m[P:, :P], 0.0).astype(bf) for m in q]
    inv = [jnp.where(eye, 1.0, jnp.where((ri >> 1) == (ci >> 1), m, 0.0)) for m in l_ab]
    for lg in range(1, 6):
        off = ((ri >> (lg + 1)) == (ci >> (lg + 1))) & (((ri >> lg) & 1) == 1) & (((ci >> lg) & 1) == 0)
        inv_b = [m.astype(bf) for m in inv]
        tmp = [mm(jnp.where(off, l_ab[i], 0.0).astype(bf), inv_b[i]) for i in range(n)]
        inv = [inv[i] + mm(inv_b[i], tmp[i].astype(bf)) for i in range(n)]
    lv = [mm(low[i], v_s[i]) for i in range(n)]
    ta = [mm(inv[i].astype(bf), jnp.concatenate([a_s[i], lv[i][:P].astype(bf)], axis=1))
          for i in range(n)]
    pa = [mm(p_rb[i], ta[i].astype(bf)) + jnp.concatenate([r_s[i], lv[i][P:]], axis=1) for i in range(n)]
    a_p = [fold(m[:, :P]) for m in ta]
    u_v = [fold(m[:, P:]) for m in ta]
    r_p = [fold(m[:, :P]).astype(bf) for m in pa]
    y_loc = [fold(m[:, P:]) for m in pa]
    zero = jnp.zeros((C, P), jnp.float32)
    gh = [mm(jnp.concatenate([cut(b_h, j, p), cut(k_h, j, p)], axis=0),
             jnp.concatenate([jnp.concatenate([a_p[i], u_v[i]], axis=1),
                              jnp.concatenate([zero, cut(vv, j, p)], axis=1)], axis=0).astype(bf), _TN)
          for i, (j, p) in enumerate(chains)]
    g_m = [(jnp.where(same, gh[i][:, :P], 0.0) + jnp.where(eye, cut(w_tot, j, p)[:1], 0.0)).astype(bf)
           for i, (j, p) in enumerate(chains)]
    h_m = [jnp.where(same, m[:, P:], 0.0) for m in gh]
    state = [s_ref[p] for p in range(n_blocks)]
    for i, (j, p) in enumerate(chains):
        s_b = state[p].astype(bf)
        y_ref[j * C:(j + 1) * C, p * P:(p + 1) * P] = mm(r_p[i], s_b) + y_loc[i]
        state[p] = mm(g_m[i], s_b) + h_m[i]
    for p in range(n_blocks):
        s_ref[p] = state[p]


def _scan(r, k, v, kk, b, lw, tt):
    T, W = r.shape
    spec = pl.BlockSpec((tt, W), lambda i: (i, 0))
    return pl.pallas_call(
        _scan_kernel,
        grid=(T // tt,),
        in_specs=[spec] * 6,
        out_specs=spec,
        out_shape=jax.ShapeDtypeStruct((T, W), jnp.float32),
        scratch_shapes=[pltpu.VMEM((W // LANES, LANES, LANES), jnp.float32)],
        compiler_params=pltpu.CompilerParams(dimension_semantics=("arbitrary",), vmem_limit_bytes=VMEM_LIMIT),
        name="wkv7_scan",
    )(r, k, v, kk, b, lw)


def _post_kernel(alpha, y_ref, r_ref, k_ref, v_ref, g_ref, ga_ref, mixb_ref, x_ref, rk_ref, gng_ref, gnb_ref,
                 hsum_ref, wba_ref, wout_ref, gate_ref, lng_ref, lnb_ref, o_ref):
    y = y_ref[...]
    hs = hsum_ref[...]
    inv_n = 1.0 / HEAD_DIM
    mu = _split_dot(y, hs) * inv_n
    yc = y - mu
    var = _split_dot(yc * yc, hs) * inv_n
    yn = yc * lax.rsqrt(var + GN_EPS) * gng_ref[...] + gnb_ref[...]
    v = v_ref[...]
    bonus = _split_dot(r_ref[...] * k_ref[...] * rk_ref[...], hs)
    ya = (yn + bonus * v) * g_ref[...]
    mixed = ga_ref[...] * _bdot(ya, wba_ref[...]) + mixb_ref[...]
    mixed = _bdot(mixed, wout_ref[...])
    xn = alpha * x_ref[...] + (1.0 + gate_ref[...]) * mixed
    o_ref[...] = _ln_rows(xn, LN_EPS) * lng_ref[...] + lnb_ref[...]


def _post(alpha, y, r, k, v, g, ga, mixb, x, gate, p, tm):
    T, D = x.shape
    W = y.shape[1]
    row = lambda w: pl.BlockSpec((tm, w), lambda i: (i, 0))
    consts = [p["r_k"], p["gn_g"], p["gn_b"], p["hsum"], p["w_ba"], p["w_out"], gate, p["ln1_g"], p["ln1_b"]]
    return pl.pallas_call(
        functools.partial(_post_kernel, alpha),
        grid=(T // tm,),
        in_specs=[row(W)] * 5 + [row(D)] * 3 + [_const_spec(a.shape) for a in consts],
        out_specs=row(D),
        out_shape=jax.ShapeDtypeStruct((T, D), jnp.float32),
        compiler_params=pltpu.CompilerParams(dimension_semantics=("arbitrary",), vmem_limit_bytes=VMEM_LIMIT),
        name="mixer_post",
    )(y, r, k, v, g, ga, mixb, x, *consts)


def _route_kernel(x_ref, sh_ref, sc_ref, rwt_ref, rb_ref, su_ref, h_ref, wt_ref, srow_ref, scol_ref, cnt_ref):
    h = _ln_rows(x_ref[...], LN_EPS) * (1.0 + sc_ref[...]) + sh_ref[...]
    h_ref[...] = h.astype(jnp.bfloat16)
    logits = lax.dot_general(rwt_ref[...], h, _NT, precision=_HI, preferred_element_type=jnp.float32)
    scores = _sigmoid(logits)
    sel = scores + rb_ref[...]
    E = sel.shape[0]
    per = E // N_EXPERT_GROUPS
    rows = [sel[e:e + 1, :] for e in range(E)]
    gscore = []
    for g in range(N_EXPERT_GROUPS):
        m = rows[g * per:(g + 1) * per]
        best = None
        for i in range(per):
            for j in range(i + 1, per):
                s2 = m[i] + m[j]
                best = s2 if best is None else jnp.maximum(best, s2)
        gscore.append(best)
    cur = gscore[0]
    bestg = jnp.zeros(cur.shape, jnp.int32)
    for g in range(1, N_EXPERT_GROUPS):
        take = gscore[g] > cur
        cur = jnp.where(take, gscore[g], cur)
        bestg = jnp.where(take, g, bestg)
    picked = []
    for e in range(E):
        g = e // per
        rank = jnp.zeros(cur.shape, jnp.int32)
        for j in range(g * per, (g + 1) * per):
            if j == e:
                continue
            ahead = (rows[j] > rows[e]) | ((rows[j] == rows[e]) & (j < e))
            rank = rank + ahead.astype(jnp.int32)
        picked.append(jnp.where((bestg == g) & (rank < 2), scores[e:e + 1, :], 0.0))
    total = picked[0]
    for e in range(1, E):
        total = total + picked[e]
    wt_ref[...] = jnp.concatenate(picked, axis=0) / total
    gsel = jnp.concatenate([jnp.where(bestg == g, 1.0, 0.0) for g in range(N_EXPERT_GROUPS)]
                           + [jnp.zeros_like(cur)] * (2 * SUBLANES - N_EXPERT_GROUPS), axis=0)
    pos = jnp.dot(gsel.astype(jnp.bfloat16), su_ref[...], preferred_element_type=jnp.float32)[:SUBLANES]
    gsel = gsel[:SUBLANES]
    cnt = jnp.sum(gsel, axis=-1, keepdims=True)
    cnt_ref[0] = jnp.broadcast_to(cnt, (SUBLANES, LANES))
    padded = jnp.ceil(cnt * (1.0 / MOE_BLOCK)) * MOE_BLOCK
    rowi = lax.broadcasted_iota(jnp.int32, (SUBLANES, 1), 0)
    offs = jnp.zeros_like(cnt)
    run = jnp.zeros((1, 1), jnp.float32)
    for g in range(1, N_EXPERT_GROUPS):
        run = run + padded[g - 1:g, :]
        offs = jnp.where(rowi == g, run, offs)
    slot = jnp.sum(gsel * (pos + offs), axis=0, keepdims=True)
    srow_ref[...] = slot
    scol_ref[...] = jnp.transpose(jnp.broadcast_to(slot, (LANES, slot.shape[1])))


def _route(x, shift, scale, router_wt, router_b, su, tm):
    T, D = x.shape
    E = router_wt.shape[0]
    f32 = jnp.float32
    return pl.pallas_call(
        _route_kernel,
        grid=(T // tm,),
        in_specs=[pl.BlockSpec((tm, D), lambda i: (i, 0)), _const_spec(shift.shape), _const_spec(scale.shape),
                  _const_spec(router_wt.shape), _const_spec(router_b.shape), _const_spec(su.shape)],
        out_specs=[pl.BlockSpec((tm, D), lambda i: (i, 0)), pl.BlockSpec((E, tm), lambda i: (0, i)),
                   pl.BlockSpec((1, tm), lambda i: (0, i)), pl.BlockSpec((tm, LANES), lambda i: (i, 0)),
                   pl.BlockSpec((1, SUBLANES, LANES), lambda i: (i, 0, 0))],
        out_shape=[jax.ShapeDtypeStruct((T, D), jnp.bfloat16), jax.ShapeDtypeStruct((E, T), f32),
                   jax.ShapeDtypeStruct((1, T), f32), jax.ShapeDtypeStruct((T, LANES), f32),
                   jax.ShapeDtypeStruct((T // tm, SUBLANES, LANES), f32)],
        compiler_params=pltpu.CompilerParams(dimension_semantics=("arbitrary",), vmem_limit_bytes=VMEM_LIMIT),
        name="moe_route",
    )(x, shift, scale, router_wt, router_b, su)


def _moe_kernel(nblk_ref, off_ref, h_ref, wt_ref, srow_ref, scol_ref, eg_ref, eu_ref, ed_ref, o_ref, acc_ref):
    g = pl.program_id(0)
    i = pl.program_id(1)
    G = pl.num_programs(0)
    tm = h_ref.shape[0]
    per, _, F = eg_ref.shape
    bf, f32 = jnp.bfloat16, jnp.float32
    dot = functools.partial(jnp.dot, preferred_element_type=f32)
    acc_ref[...] = jnp.zeros_like(acc_ref)
    wt = wt_ref[...]
    w_hi = wt.astype(bf)
    w_r = wt - w_hi.astype(f32)
    w_mid = w_r.astype(bf)
    w_lo = (w_r - w_mid.astype(f32)).astype(bf)
    first = off_ref[i * G + g].astype(f32)

    def block(j, carry):
        base = first + (j * MOE_BLOCK).astype(f32)
        srow = lax.broadcasted_iota(jnp.int32, (MOE_BLOCK, tm), 0).astype(f32) + base
        sel = jnp.where(srow == srow_ref[...], 1.0, 0.0).astype(bf)
        xb = dot(sel, h_ref[...]).astype(bf)
        wsb = dot(sel, w_hi) + dot(sel, w_mid) + dot(sel, w_lo)
        lane = lax.broadcasted_iota(jnp.int32, wsb.shape, 1)
        out = None
        for e in range(per):
            wcol = jnp.sum(jnp.where(lane == g * per + e, wsb, 0.0), axis=-1, keepdims=True)
            gt = dot(xb, eg_ref[e])
            hid = gt * _sigmoid(gt) * dot(xb, eu_ref[e]) * wcol
            part = dot(hid.astype(bf), ed_ref[e])
            out = part if out is None else out + part
        scol = lax.broadcasted_iota(jnp.int32, (tm, LANES), 1).astype(f32) + base
        back = jnp.where(scol == scol_ref[...], 1.0, 0.0).astype(bf)
        acc_ref[...] += dot(back, out.astype(bf))
        return carry

    lax.fori_loop(0, nblk_ref[i * G + g], block, 0)
    o_ref[0] = acc_ref[...].astype(bf)


def _combine_kernel(alpha, x_ref, y_ref, gate_ref, lng_ref, lnb_ref, o_ref):
    y = y_ref[0].astype(jnp.float32)
    for g in range(1, y_ref.shape[0]):
        y = y + y_ref[g].astype(jnp.float32)
    xn = alpha * x_ref[...] + (1.0 + gate_ref[...]) * y
    o_ref[...] = _ln_rows(xn, LN_EPS) * lng_ref[...] + lnb_ref[...]


def _moe(alpha, x, h, wt, srow, scol, cnt, eg, eu, ed, gate, ln_g, ln_b, tm, tm_out):
    T, D = x.shape
    E, _, F = eg.shape
    G = N_EXPERT_GROUPS
    per = E // G
    nt = T // tm
    cnt_i = cnt[:, :G, 0].astype(jnp.int32)
    nblk = (cnt_i + MOE_BLOCK - 1) // MOE_BLOCK
    off = (jnp.cumsum(nblk, axis=1) - nblk) * MOE_BLOCK
    tok = lambda w: pl.BlockSpec((tm, w), lambda g, i, *_: (i, 0))
    grp = lambda a, b: pl.BlockSpec((per, a, b), lambda g, i, *_: (g, 0, 0), pipeline_mode=pl.Buffered(1))
    yg = pl.pallas_call(
        _moe_kernel,
        grid_spec=pltpu.PrefetchScalarGridSpec(
            num_scalar_prefetch=2,
            grid=(G, nt),
            in_specs=[tok(D), tok(E), pl.BlockSpec((1, tm), lambda g, i, *_: (0, i)), tok(LANES),
                      grp(D, F), grp(D, F), grp(F, D)],
            out_specs=pl.BlockSpec((1, tm, D), lambda g, i, *_: (g, i, 0)),
            scratch_shapes=[pltpu.VMEM((tm, D), jnp.float32)]),
        out_shape=jax.ShapeDtypeStruct((G, T, D), jnp.bfloat16),
        compiler_params=pltpu.CompilerParams(dimension_semantics=("arbitrary", "arbitrary"),
                                             vmem_limit_bytes=VMEM_LIMIT),
        name="moe_experts",
    )(nblk.reshape(-1), off.reshape(-1), h, wt, srow, scol, eg, eu, ed)
    vec = pl.BlockSpec((1, D), lambda i: (0, 0))
    return pl.pallas_call(
        functools.partial(_combine_kernel, alpha),
        grid=(T // tm_out,),
        in_specs=[pl.BlockSpec((tm_out, D), lambda i: (i, 0)), pl.BlockSpec((G, tm_out, D), lambda i: (0, i, 0)),
                  vec, vec, vec],
        out_specs=pl.BlockSpec((tm_out, D), lambda i: (i, 0)),
        out_shape=jax.ShapeDtypeStruct((T, D), jnp.float32),
        compiler_params=pltpu.CompilerParams(dimension_semantics=("arbitrary",), vmem_limit_bytes=VMEM_LIMIT),
        name="moe_combine",
    )(x, yg, gate, ln_g, ln_b)


def kernel(x, c, w_mod, b_mod, w_in, rwkv_mu, w0, w2, a0, a2, g2, k_k, k_a, r_k, gn_g, gn_b, vm0, vm1, vm2,
           sg_ln_g, sg_ln_b, w_s, b_s, w_branch_a, w_branch_b, w_out, ln1_g, ln1_b, router_w, router_b,
           e_gate, e_up, e_down, ln2_g, ln2_b):
    B, T, D = x.shape
    assert B == 1
    L = w_mod.shape[0]
    W = w0.shape[1]
    S = sg_ln_g.shape[1]
    dl, al = w2.shape[1], a2.shape[1]
    assert dl + al == LANES and W % LANES == 0 and S % LANES == 0
    rw_cols = rwkv_mu.shape[1]
    G, CH = w_s.shape[1], w_s.shape[2]
    assert CH == LANES and S // G == HEAD_DIM and W // r_k.shape[1] == HEAD_DIM
    alpha = float((2 * L) ** 0.25)
    bf = jnp.bfloat16
    f32 = jnp.float32

    mod = _modulation(c, w_mod, b_mod)
    head = jnp.arange(W) // HEAD_DIM
    hsum = (head[:, None] == head[None, :]).astype(bf)
    causal = jnp.tril(jnp.ones((CH, CH), bool))
    router_wt = router_w.T
    router_bc = router_b.reshape(-1, 1)
    tidx = jnp.arange(MOE_TILE)
    su = (tidx[:, None] < tidx[None, :]).astype(bf)

    xt = x.reshape(T, D)
    v_first = None
    for i in range(L):
        sh1, sc1, gt1, sh2, sc2, gt2 = [mod[i, :, j * D:(j + 1) * D] for j in range(6)]
        w_m = jnp.where(causal, w_s[i], 0.0)
        p = dict(
            w_rw=w_in[i, :, :rw_cols].astype(bf),
            w_sg=w_in[i, :, rw_cols:rw_cols + 2 * S].astype(bf),
            w_gt=w_in[i, :, rw_cols + 2 * S:].astype(bf),
            mu=rwkv_mu[i][None], w0=w0[i][None], a0=a0[i][None],
            w2p=jnp.concatenate([w2[i], jnp.zeros((al, W), f32)], 0).astype(bf),
            a2p=jnp.concatenate([jnp.zeros((dl, W), f32), a2[i]], 0).astype(bf),
            g2=g2[i].astype(bf), k_k=k_k[i][None], k_a=k_a[i][None], hsum=hsum,
            sg_ln_g=sg_ln_g[i][None], sg_ln_b=sg_ln_b[i][None],
            wpair=jnp.concatenate([w_m[0::2], w_m[1::2]], axis=2).astype(bf),
            sbias=jnp.repeat(b_s[i].T, HEAD_DIM, axis=1),
            w_bb=w_branch_b[i].astype(bf), w_ba=w_branch_a[i].astype(bf), w_out=w_out[i].astype(bf),
            r_k=r_k[i].reshape(1, W), gn_g=gn_g[i][None], gn_b=gn_b[i][None],
            ln1_g=ln1_g[i][None], ln1_b=ln1_b[i][None],
        )
        if i > 0:
            p.update(vm0=vm0[i - 1][None], vm1=vm1[i - 1].astype(bf), vm2=vm2[i - 1].astype(bf))
        r, k, v, kk, b, lw, g, ga, mixb = _front(xt, sh1, sc1, p, v_first, tm=256)
        if i == 0:
            v_first = v
        y = _scan(r, k, v, kk, b, lw, tt=256)
        xt = _post(alpha, y, r, k, v, g, ga, mixb, xt, gt1, p, tm=256)
        h, wt, srow, scol, cnt = _route(xt, sh2, sc2, router_wt, router_bc, su, tm=MOE_TILE)
        xt = _moe(alpha, xt, h, wt.T, srow, scol, cnt, e_gate[i].astype(bf), e_up[i].astype(bf),
                  e_down[i].astype(bf), gt2, ln2_g[i][None], ln2_b[i][None], tm=MOE_TILE, tm_out=512)
    return xt.reshape(B, T, D)
```

```python
import functools

import jax
import jax.numpy as jnp
from jax import lax
from jax.experimental import pallas as pl
from jax.experimental.pallas import tpu as pltpu

LN_EPS = 1e-5
GN_EPS = 64e-5
N_EXPERT_GROUPS = 4
LANES = 128
SUBLANES = 8
MOE_BLOCK = 128
SCAN_CHUNK = 64
HEAD_DIM = 64
MOE_TILE = 1024
VMEM_LIMIT = 56 * 1024 * 1024

_HI = lax.Precision.HIGHEST
_NT = (((1,), (1,)), ((), ()))
_TN = (((0,), (0,)), ((), ()))


def _bdot(a, b):
    return jnp.dot(a.astype(jnp.bfloat16), b.astype(jnp.bfloat16), preferred_element_type=jnp.float32)


def _split_dot(a, b_exact, terms=2):
    acc = None
    rem = a
    for _ in range(terms):
        piece = rem.astype(jnp.bfloat16)
        part = jnp.dot(piece, b_exact, preferred_element_type=jnp.float32)
        acc = part if acc is None else acc + part
        rem = rem - piece.astype(jnp.float32)
    return acc


def _sigmoid(x):
    return 1.0 / (1.0 + jnp.exp(-x))


def _softplus(x):
    return jnp.maximum(x, 0.0) + jnp.log(1.0 + jnp.exp(-jnp.abs(x)))


def _ln_rows(x, eps):
    mu = jnp.mean(x, axis=-1, keepdims=True)
    xc = x - mu
    var = jnp.mean(xc * xc, axis=-1, keepdims=True)
    return xc * lax.rsqrt(var + eps)


def _combine_rows(alpha, x, yg_ref, gate_ref, lng_ref, lnb_ref):
    y = yg_ref[0].astype(jnp.float32)
    for g in range(1, yg_ref.shape[0]):
        y = y + yg_ref[g].astype(jnp.float32)
    xn = alpha * x + (1.0 + gate_ref[...]) * y
    return _ln_rows(xn, LN_EPS) * lng_ref[...] + lnb_ref[...]


def _const_spec(shape):
    nd = len(shape)
    return pl.BlockSpec(shape, lambda *_: (0,) * nd, pipeline_mode=pl.Buffered(1))


def _mod_kernel(c_ref, w_ref, b_ref, o_ref):
    c = c_ref[...]
    cond = c * _sigmoid(c)
    o_ref[0] = jnp.sum(cond * w_ref[0], axis=0, keepdims=True) + b_ref[0]


def _modulation(c, w_mod, b_mod):
    L, D, D6 = w_mod.shape
    nb = D6 // D
    out = pl.pallas_call(
        _mod_kernel,
        grid=(L, nb),
        in_specs=[pl.BlockSpec((D, 1), lambda l, j: (0, 0)),
                  pl.BlockSpec((1, D, D), lambda l, j: (l, 0, j)),
                  pl.BlockSpec((1, 1, D), lambda l, j: (l, 0, j))],
        out_specs=pl.BlockSpec((1, 1, D), lambda l, j: (l, 0, j)),
        out_shape=jax.ShapeDtypeStruct((L, 1, D6), jnp.float32),
        compiler_params=pltpu.CompilerParams(dimension_semantics=("arbitrary", "arbitrary"),
                                             vmem_limit_bytes=VMEM_LIMIT),
        name="adaln_mod",
    )(c.reshape(D, 1), w_mod, b_mod.reshape(L, 1, D6))
    return out


def _front_kernel(has_vmix, prev_alpha, W, *refs):
    x_ref = refs[0]
    pos = 1
    if prev_alpha is not None:
        yg_ref, pgate_ref, plng_ref, plnb_ref = refs[pos:pos + 4]
        pos += 4
    (sh_ref, sc_ref, wrw_ref, wsg_ref, wgt_ref, mu_ref, w0_ref, w2p_ref, a0_ref, a2p_ref, g2_ref,
     kkw_ref, ka_ref, hsum_ref) = refs[pos:pos + 14]
    pos += 14
    if has_vmix:
        vf_ref, vm0_ref, vm1_ref, vm2_ref = refs[pos:pos + 4]
        pos += 4
    lng_ref, lnb_ref, wpair_ref, sbias_ref, wbb_ref = refs[pos:pos + 5]
    pos += 5
    r_o, k_o, v_o, kk_o, b_o, lw_o, g_o, ga_o, mixb_o = refs[pos:pos + 9]
    pos += 9
    if prev_alpha is not None:
        x_o = refs[pos]
        pos += 1
    carry_ref = refs[pos]

    tm = x_ref.shape[0]

    @pl.when(pl.program_id(0) == 0)
    def _():
        carry_ref[...] = jnp.zeros_like(carry_ref)

    x = x_ref[...]
    if prev_alpha is not None:
        x = _combine_rows(prev_alpha, x, yg_ref, pgate_ref, plng_ref, plnb_ref)
        x_o[...] = x
    h = _ln_rows(x, LN_EPS) * (1.0 + sc_ref[...]) + sh_ref[...]
    hb = h.astype(jnp.bfloat16)

    z = jnp.dot(hb, wrw_ref[...], preferred_element_type=jnp.float32)
    row = lax.broadcasted_iota(jnp.int32, z.shape, 0)
    prev = jnp.where(row == 0, carry_ref[...], pltpu.roll(z, 1, axis=0))
    carry_ref[...] = z[tm - 1:tm, :]
    z = z + (prev - z) * mu_ref[...]
    r = z[:, :W]
    k = z[:, W:2 * W]
    v = z[:, 2 * W:3 * W]
    zwa = z[:, 3 * W:3 * W + LANES]
    zg = z[:, 3 * W + LANES:]
    w_log = -_softplus(-(w0_ref[...] + _bdot(jnp.tanh(zwa), w2p_ref[...]))) - 0.5
    lw_o[...] = -jnp.exp(w_log)
    a = _sigmoid(a0_ref[...] + _bdot(zwa, a2p_ref[...]))
    g_o[...] = _bdot(_sigmoid(zg), g2_ref[...])
    if has_vmix:
        mix = _sigmoid(vm0_ref[...] + _bdot(_bdot(v, vm1_ref[...]), vm2_ref[...]))
        v = v + (vf_ref[...] - v) * mix
    kk = k * kkw_ref[...]
    ss = _split_dot(kk * kk, hsum_ref[...])
    kk = kk / jnp.maximum(jnp.sqrt(ss), 1e-12)
    r_o[...] = r
    k_o[...] = k * (1.0 + (a - 1.0) * ka_ref[...])
    v_o[...] = v
    kk_o[...] = kk
    b_o[...] = kk * a

    zs = jnp.dot(hb, wsg_ref[...], preferred_element_type=jnp.float32)
    zs = zs * (0.5 * (1.0 + jnp.tanh(0.7978845608028654 * (zs + 0.044715 * (zs * zs * zs)))))
    S = zs.shape[1] // 2
    u = zs[:, :S]
    vln = _ln_rows(zs[:, S:], LN_EPS) * lng_ref[...] + lnb_ref[...]
    lo = lax.broadcasted_iota(jnp.int32, (LANES, LANES), 1) < HEAD_DIM
    chunks = []
    for c in range(tm // LANES):
        cols = []
        for p in range(S // LANES):
            vp = vln[c * LANES:(c + 1) * LANES, p * LANES:(p + 1) * LANES]
            vs = jnp.concatenate([jnp.where(lo, vp, 0.0), jnp.where(lo, 0.0, vp)], axis=0)
            cols.append(_bdot(wpair_ref[p], vs))
        chunks.append(jnp.concatenate(cols, axis=1) + sbias_ref[...])
    yb = u * jnp.concatenate(chunks, axis=0)

    zg2 = jnp.dot(hb, wgt_ref[...], preferred_element_type=jnp.float32)
    Dm = zg2.shape[1] // 2
    ga_o[...] = _sigmoid(zg2[:, :Dm])
    mixb_o[...] = _sigmoid(zg2[:, Dm:]) * _bdot(yb, wbb_ref[...])


def _front(x, shift, scale, p, v_first, tm, prev=None):
    T, D = x.shape
    W = p["hsum"].shape[0]
    has_vmix = v_first is not None
    row = lambda w: pl.BlockSpec((tm, w), lambda i: (i, 0))
    ins = [x]
    specs = [row(D)]
    if prev is not None:
        ins += list(prev[1:])
        specs += [pl.BlockSpec((prev[1].shape[0], tm, D), lambda i: (0, i, 0))] + [_const_spec(a.shape)
                                                                                   for a in prev[2:]]
    consts = [shift, scale, p["w_rw"], p["w_sg"], p["w_gt"], p["mu"], p["w0"], p["w2p"], p["a0"], p["a2p"], p["g2"],
              p["k_k"], p["k_a"], p["hsum"]]
    ins += consts
    specs += [_const_spec(a.shape) for a in consts]
    if has_vmix:
        extra = [v_first, p["vm0"], p["vm1"], p["vm2"]]
        ins += extra
        specs += [row(W)] + [_const_spec(a.shape) for a in extra[1:]]
    tail = [p["sg_ln_g"], p["sg_ln_b"], p["wpair"], p["sbias"], p["w_bb"]]
    ins += tail
    specs += [_const_spec(a.shape) for a in tail]
    f32 = jnp.float32
    n_wide = 2 if prev is None else 3
    out_shape = [jax.ShapeDtypeStruct((T, W), f32)] * 7 + [jax.ShapeDtypeStruct((T, D), f32)] * n_wide
    out_specs = [row(W)] * 7 + [row(D)] * n_wide
    return pl.pallas_call(
        functools.partial(_front_kernel, has_vmix, None if prev is None else prev[0], W),
        grid=(T // tm,),
        in_specs=specs,
        out_specs=out_specs,
        out_shape=out_shape,
        scratch_shapes=[pltpu.VMEM((1, p["w_rw"].shape[1]), f32)],
        compiler_params=pltpu.CompilerParams(dimension_semantics=("arbitrary",), vmem_limit_bytes=VMEM_LIMIT),
        name="mixer_front",
    )(*ins)


def _scan_kernel(r_ref, k_ref, v_ref, kk_ref, b_ref, lw_ref, y_ref, s_ref):
    C = SCAN_CHUNK
    P = LANES
    bf = jnp.bfloat16
    tt, width = lw_ref.shape
    n_chunks = tt // C
    n_blocks = width // P

    @pl.when(pl.program_id(0) == 0)
    def _():
        s_ref[...] = jnp.zeros_like(s_ref)

    lw = lw_ref[...]
    ti = lax.broadcasted_iota(jnp.int32, (tt, tt), 0)
    tj = lax.broadcasted_iota(jnp.int32, (tt, tt), 1)
    same_chunk = (ti // C) == (tj // C)
    pre = jnp.concatenate([jnp.where(same_chunk & (ti >= tj), 1.0, 0.0),
                           jnp.where(same_chunk, 1.0, 0.0)], axis=0).astype(bf)
    sums = None
    rem = lw
    for _ in range(3):
        piece = rem.astype(bf)
        part = jnp.dot(pre, piece, preferred_element_type=jnp.float32)
        sums = part if sums is None else sums + part
        rem = rem - piece.astype(jnp.float32)
    cum, tot = sums[:tt], sums[tt:]
    w_inv = jnp.exp(-cum)
    w_end = jnp.exp(tot - cum)
    w_tot = jnp.exp(tot)

    kk = kk_ref[...]
    bb = b_ref[...]
    kx = k_ref[...]
    a_t = -kk * jnp.exp(cum - lw)
    b_t = (bb * w_inv).astype(bf)
    k_t = (kx * w_inv).astype(bf)
    r_t = r_ref[...] * jnp.exp(cum)
    b_h = (bb * w_end).astype(bf)
    k_h = (kx * w_end).astype(bf)
    vv = v_ref[...]

    ri = lax.broadcasted_iota(jnp.int32, (P, P), 0)
    ci = lax.broadcasted_iota(jnp.int32, (P, P), 1)
    same = (ri >= C) == (ci >= C)
    strict = same & (ri > ci)
    incl = same & (ri >= ci)
    eye = ri == ci
    lo = lax.broadcasted_iota(jnp.int32, (C, P), 1) < HEAD_DIM

    def stack(m):
        return jnp.concatenate([jnp.where(lo, m, 0.0), jnp.where(lo, 0.0, m)], axis=0)

    def fold(m):
        return m[:C] + m[C:]

    def mm(a, b, dims=(((1,), (0,)), ((), ()))):
        return lax.dot_general(a, b, dims, preferred_element_type=jnp.float32)

    chains = [(j, p) for j in range(n_chunks) for p in range(n_blocks)]
    n = len(chains)
    cut = lambda m, j, p: m[j * C:(j + 1) * C, p * P:(p + 1) * P]
    a_s = [stack(cut(a_t, j, p)).astype(bf) for j, p in chains]
    r_s = [stack(cut(r_t, j, p)) for j, p in chains]
    v_s = [stack(cut(vv, j, p)).astype(bf) for j, p in chains]
    q = [mm(jnp.concatenate([a_s[i], r_s[i].astype(bf)], axis=0),
            jnp.concatenate([cut(b_t, j, p)] * 2 + [cut(k_t, j, p)] * 2, axis=0), _NT)
         for i, (j, p) in enumerate(chains)]
    l_ab = [jnp.where(strict, m[:P, :P], 0.0) for m in q]
    low = [jnp.concatenate([jnp.where(strict, m[:P, P:], 0.0), jnp.where(incl, m[P:, P:], 0.0)], axis=0).astype(bf)
           for m in q]
    p_rb = [jnp.where(incl, m[P:, :P], 0.0).astype(bf) for m in q]
    inv = [jnp.where(eye, 1.0, jnp.where((ri >> 1) == (ci >> 1), m, 0.0)) for m in l_ab]
    for lg in range(1, 6):
        off = ((ri >> (lg + 1)) == (ci >> (lg + 1))) & (((ri >> lg) & 1) == 1) & (((ci >> lg) & 1) == 0)
        inv_b = [m.astype(bf) for m in inv]
        tmp = [mm(jnp.where(off, l_ab[i], 0.0).astype(bf), inv_b[i]) for i in range(n)]
        inv = [inv[i] + mm(inv_b[i], tmp[i].astype(bf)) for i in range(n)]
    lv = [mm(low[i], v_s[i]) for i in range(n)]
    ta = [mm(inv[i].astype(bf), jnp.concatenate([a_s[i], lv[i][:P].astype(bf)], axis=1))
          for i in range(n)]
    pa = [mm(p_rb[i], ta[i].astype(bf)) + jnp.concatenate([r_s[i], lv[i][P:]], axis=1) for i in range(n)]
    a_p = [fold(m[:, :P]) for m in ta]
    u_v = [fold(m[:, P:]) for m in ta]
    r_p = [fold(m[:, :P]).astype(bf) for m in pa]
    y_loc = [fold(m[:, P:]) for m in pa]
    zero = jnp.zeros((C, P), jnp.float32)
    gh = [mm(jnp.concatenate([cut(b_h, j, p), cut(k_h, j, p)], axis=0),
             jnp.concatenate([jnp.concatenate([a_p[i], u_v[i]], axis=1),
                              jnp.concatenate([zero, cut(vv, j, p)], axis=1)], axis=0).astype(bf), _TN)
          for i, (j, p) in enumerate(chains)]
    g_m = [(jnp.where(same, gh[i][:, :P], 0.0) + jnp.where(eye, cut(w_tot, j, p)[:1], 0.0)).astype(bf)
           for i, (j, p) in enumerate(chains)]
    h_m = [jnp.where(same, m[:, P:], 0.0) for m in gh]
    state = [s_ref[p] for p in range(n_blocks)]
    for i, (j, p) in enumerate(chains):
        s_b = state[p].astype(bf)
        y_ref[j * C:(j + 1) * C, p * P:(p + 1) * P] = mm(r_p[i], s_b) + y_loc[i]
        state[p] = mm(g_m[i], s_b) + h_m[i]
    for p in range(n_blocks):
        s_ref[p] = state[p]


def _scan(r, k, v, kk, b, lw, tt):
    T, W = r.shape
    spec = pl.BlockSpec((tt, W), lambda i: (i, 0))
    return pl.pallas_call(
        _scan_kernel,
        grid=(T // tt,),
        in_specs=[spec] * 6,
        out_specs=spec,
        out_shape=jax.ShapeDtypeStruct((T, W), jnp.float32),
        scratch_shapes=[pltpu.VMEM((W // LANES, LANES, LANES), jnp.float32)],
        compiler_params=pltpu.CompilerParams(dimension_semantics=("arbitrary",), vmem_limit_bytes=VMEM_LIMIT),
        name="wkv7_scan",
    )(r, k, v, kk, b, lw)


def _post_kernel(alpha, y_ref, r_ref, k_ref, v_ref, g_ref, ga_ref, mixb_ref, x_ref, rk_ref, gng_ref, gnb_ref,
                 hsum_ref, wba_ref, wout_ref, gate_ref, lng_ref, lnb_ref, o_ref):
    y = y_ref[...]
    hs = hsum_ref[...]
    inv_n = 1.0 / HEAD_DIM
    mu = _split_dot(y, hs) * inv_n
    yc = y - mu
    var = _split_dot(yc * yc, hs) * inv_n
    yn = yc * lax.rsqrt(var + GN_EPS) * gng_ref[...] + gnb_ref[...]
    v = v_ref[...]
    bonus = _split_dot(r_ref[...] * k_ref[...] * rk_ref[...], hs)
    ya = (yn + bonus * v) * g_ref[...]
    mixed = ga_ref[...] * _bdot(ya, wba_ref[...]) + mixb_ref[...]
    mixed = _bdot(mixed, wout_ref[...])
    xn = alpha * x_ref[...] + (1.0 + gate_ref[...]) * mixed
    o_ref[...] = _ln_rows(xn, LN_EPS) * lng_ref[...] + lnb_ref[...]


def _post(alpha, y, r, k, v, g, ga, mixb, x, gate, p, tm):
    T, D = x.shape
    W = y.shape[1]
    row = lambda w: pl.BlockSpec((tm, w), lambda i: (i, 0))
    consts = [p["r_k"], p["gn_g"], p["gn_b"], p["hsum"], p["w_ba"], p["w_out"], gate, p["ln1_g"], p["ln1_b"]]
    return pl.pallas_call(
        functools.partial(_post_kernel, alpha),
        grid=(T // tm,),
        in_specs=[row(W)] * 5 + [row(D)] * 3 + [_const_spec(a.shape) for a in consts],
        out_specs=row(D),
        out_shape=jax.ShapeDtypeStruct((T, D), jnp.float32),
        compiler_params=pltpu.CompilerParams(dimension_semantics=("arbitrary",), vmem_limit_bytes=VMEM_LIMIT),
        name="mixer_post",
    )(y, r, k, v, g, ga, mixb, x, *consts)


def _route_kernel(x_ref, sh_ref, sc_ref, rwt_ref, rb_ref, su_ref, h_ref, wt_ref, srow_ref, scol_ref, cnt_ref):
    h = _ln_rows(x_ref[...], LN_EPS) * (1.0 + sc_ref[...]) + sh_ref[...]
    h_ref[...] = h.astype(jnp.bfloat16)
    logits = lax.dot_general(rwt_ref[...], h, _NT, precision=_HI, preferred_element_type=jnp.float32)
    scores = _sigmoid(logits)
    sel = scores + rb_ref[...]
    E = sel.shape[0]
    per = E // N_EXPERT_GROUPS
    rows = [sel[e:e + 1, :] for e in range(E)]
    gscore = []
    for g in range(N_EXPERT_GROUPS):
        m = rows[g * per:(g + 1) * per]
        best = None
        for i in range(per):
            for j in range(i + 1, per):
                s2 = m[i] + m[j]
                best = s2 if best is None else jnp.maximum(best, s2)
        gscore.append(best)
    cur = gscore[0]
    bestg = jnp.zeros(cur.shape, jnp.int32)
    for g in range(1, N_EXPERT_GROUPS):
        take = gscore[g] > cur
        cur = jnp.where(take, gscore[g], cur)
        bestg = jnp.where(take, g, bestg)
    picked = []
    for e in range(E):
        g = e // per
        rank = jnp.zeros(cur.shape, jnp.int32)
        for j in range(g * per, (g + 1) * per):
            if j == e:
                continue
            ahead = (rows[j] > rows[e]) | ((rows[j] == rows[e]) & (j < e))
            rank = rank + ahead.astype(jnp.int32)
        picked.append(jnp.where((bestg == g) & (rank < 2), scores[e:e + 1, :], 0.0))
    total = picked[0]
    for e in range(1, E):
        total = total + picked[e]
    wt_ref[...] = jnp.concatenate(picked, axis=0) / total
    gsel = jnp.concatenate([jnp.where(bestg == g, 1.0, 0.0) for g in range(N_EXPERT_GROUPS)]
                           + [jnp.zeros_like(cur)] * (2 * SUBLANES - N_EXPERT_GROUPS), axis=0)
    pos = jnp.dot(gsel.astype(jnp.bfloat16), su_ref[...], preferred_element_type=jnp.float32)[:SUBLANES]
    gsel = gsel[:SUBLANES]
    cnt = jnp.sum(gsel, axis=-1, keepdims=True)
    cnt_ref[0] = jnp.broadcast_to(cnt, (SUBLANES, LANES))
    padded = jnp.ceil(cnt * (1.0 / MOE_BLOCK)) * MOE_BLOCK
    rowi = lax.broadcasted_iota(jnp.int32, (SUBLANES, 1), 0)
    offs = jnp.zeros_like(cnt)
    run = jnp.zeros((1, 1), jnp.float32)
    for g in range(1, N_EXPERT_GROUPS):
        run = run + padded[g - 1:g, :]
        offs = jnp.where(rowi == g, run, offs)
    slot = jnp.sum(gsel * (pos + offs), axis=0, keepdims=True)
    srow_ref[...] = slot
    scol_ref[...] = jnp.transpose(jnp.broadcast_to(slot, (LANES, slot.shape[1])))


def _route(x, shift, scale, router_wt, router_b, su, tm):
    T, D = x.shape
    E = router_wt.shape[0]
    f32 = jnp.float32
    return pl.pallas_call(
        _route_kernel,
        grid=(T // tm,),
        in_specs=[pl.BlockSpec((tm, D), lambda i: (i, 0)), _const_spec(shift.shape), _const_spec(scale.shape),
                  _const_spec(router_wt.shape), _const_spec(router_b.shape), _const_spec(su.shape)],
        out_specs=[pl.BlockSpec((tm, D), lambda i: (i, 0)), pl.BlockSpec((E, tm), lambda i: (0, i)),
                   pl.BlockSpec((1, tm), lambda i: (0, i)), pl.BlockSpec((tm, LANES), lambda i: (i, 0)),
                   pl.BlockSpec((1, SUBLANES, LANES), lambda i: (i, 0, 0))],
        out_shape=[jax.ShapeDtypeStruct((T, D), jnp.bfloat16), jax.ShapeDtypeStruct((E, T), f32),
                   jax.ShapeDtypeStruct((1, T), f32), jax.ShapeDtypeStruct((T, LANES), f32),
                   jax.ShapeDtypeStruct((T // tm, SUBLANES, LANES), f32)],
        compiler_params=pltpu.CompilerParams(dimension_semantics=("arbitrary",), vmem_limit_bytes=VMEM_LIMIT),
        name="moe_route",
    )(x, shift, scale, router_wt, router_b, su)


def _moe_kernel(nblk_ref, off_ref, h_ref, wt_ref, srow_ref, scol_ref, eg_ref, eu_ref, ed_ref, o_ref, acc_ref):
    g = pl.program_id(0)
    i = pl.program_id(1)
    G = pl.num_programs(0)
    tm = h_ref.shape[0]
    per, _, F = eg_ref.shape
    bf, f32 = jnp.bfloat16, jnp.float32
    dot = functools.partial(jnp.dot, preferred_element_type=f32)
    acc_ref[...] = jnp.zeros_like(acc_ref)
    wt = wt_ref[...]
    w_hi = wt.astype(bf)
    w_r = wt - w_hi.astype(f32)
    w_mid = w_r.astype(bf)
    w_lo = (w_r - w_mid.astype(f32)).astype(bf)
    w_parts = jnp.concatenate([w_hi, w_mid, w_lo], axis=1)
    E = wt.shape[1]
    first = off_ref[i * G + g].astype(f32)

    def block(j, carry):
        base = first + (j * MOE_BLOCK).astype(f32)
        srow = lax.broadcasted_iota(jnp.int32, (MOE_BLOCK, tm), 0).astype(f32) + base
        sel = jnp.where(srow == srow_ref[...], 1.0, 0.0).astype(bf)
        xb = dot(sel, h_ref[...]).astype(bf)
        wsp = dot(sel, w_parts)
        wsb = wsp[:, :E] + wsp[:, E:2 * E] + wsp[:, 2 * E:]
        lane = lax.broadcasted_iota(jnp.int32, wsb.shape, 1)
        out = None
        for e in range(per):
            wcol = jnp.sum(jnp.where(lane == g * per + e, wsb, 0.0), axis=-1, keepdims=True)
            gt = dot(xb, eg_ref[e])
            hid = gt * _sigmoid(gt) * dot(xb, eu_ref[e]) * wcol
            part = dot(hid.astype(bf), ed_ref[e])
            out = part if out is None else out + part
        scol = lax.broadcasted_iota(jnp.int32, (tm, LANES), 1).astype(f32) + base
        back = jnp.where(scol == scol_ref[...], 1.0, 0.0).astype(bf)
        acc_ref[...] += dot(back, out.astype(bf))
        return carry

    lax.fori_loop(0, nblk_ref[i * G + g], block, 0)
    o_ref[0] = acc_ref[...].astype(bf)


def _combine_kernel(alpha, x_ref, y_ref, gate_ref, lng_ref, lnb_ref, o_ref):
    o_ref[...] = _combine_rows(alpha, x_ref[...], y_ref, gate_ref, lng_ref, lnb_ref)


def _moe(h, wt, srow, scol, cnt, eg, eu, ed, tm):
    T, D = h.shape
    E, _, F = eg.shape
    G = N_EXPERT_GROUPS
    per = E // G
    nt = T // tm
    cnt_i = cnt[:, :G, 0].astype(jnp.int32)
    nblk = (cnt_i + MOE_BLOCK - 1) // MOE_BLOCK
    off = (jnp.cumsum(nblk, axis=1) - nblk) * MOE_BLOCK
    tok = lambda w: pl.BlockSpec((tm, w), lambda g, i, *_: (i, 0))
    grp = lambda a, b: pl.BlockSpec((per, a, b), lambda g, i, *_: (g, 0, 0), pipeline_mode=pl.Buffered(1))
    return pl.pallas_call(
        _moe_kernel,
        grid_spec=pltpu.PrefetchScalarGridSpec(
            num_scalar_prefetch=2,
            grid=(G, nt),
            in_specs=[tok(D), tok(E), pl.BlockSpec((1, tm), lambda g, i, *_: (0, i)), tok(LANES),
                      grp(D, F), grp(D, F), grp(F, D)],
            out_specs=pl.BlockSpec((1, tm, D), lambda g, i, *_: (g, i, 0)),
            scratch_shapes=[pltpu.VMEM((tm, D), jnp.float32)]),
        out_shape=jax.ShapeDtypeStruct((G, T, D), jnp.bfloat16),
        compiler_params=pltpu.CompilerParams(dimension_semantics=("arbitrary", "arbitrary"),
                                             vmem_limit_bytes=VMEM_LIMIT),
        name="moe_experts",
    )(nblk.reshape(-1), off.reshape(-1), h, wt, srow, scol, eg, eu, ed)


def _combine(alpha, x, yg, gate, ln_g, ln_b, tm):
    T, D = x.shape
    vec = pl.BlockSpec((1, D), lambda i: (0, 0))
    return pl.pallas_call(
        functools.partial(_combine_kernel, alpha),
        grid=(T // tm,),
        in_specs=[pl.BlockSpec((tm, D), lambda i: (i, 0)), pl.BlockSpec((yg.shape[0], tm, D), lambda i: (0, i, 0)),
                  vec, vec, vec],
        out_specs=pl.BlockSpec((tm, D), lambda i: (i, 0)),
        out_shape=jax.ShapeDtypeStruct((T, D), jnp.float32),
        compiler_params=pltpu.CompilerParams(dimension_semantics=("arbitrary",), vmem_limit_bytes=VMEM_LIMIT),
        name="moe_combine",
    )(x, yg, gate, ln_g, ln_b)


def kernel(x, c, w_mod, b_mod, w_in, rwkv_mu, w0, w2, a0, a2, g2, k_k, k_a, r_k, gn_g, gn_b, vm0, vm1, vm2,
           sg_ln_g, sg_ln_b, w_s, b_s, w_branch_a, w_branch_b, w_out, ln1_g, ln1_b, router_w, router_b,
           e_gate, e_up, e_down, ln2_g, ln2_b):
    B, T, D = x.shape
    assert B == 1
    L = w_mod.shape[0]
    W = w0.shape[1]
    S = sg_ln_g.shape[1]
    dl, al = w2.shape[1], a2.shape[1]
    assert dl + al == LANES and W % LANES == 0 and S % LANES == 0
    rw_cols = rwkv_mu.shape[1]
    G, CH = w_s.shape[1], w_s.shape[2]
    assert CH == LANES and S // G == HEAD_DIM and W // r_k.shape[1] == HEAD_DIM
    alpha = float((2 * L) ** 0.25)
    bf = jnp.bfloat16
    f32 = jnp.float32

    mod = _modulation(c, w_mod, b_mod)
    head = jnp.arange(W) // HEAD_DIM
    hsum = (head[:, None] == head[None, :]).astype(bf)
    causal = jnp.tril(jnp.ones((CH, CH), bool))
    router_wt = router_w.T
    router_bc = router_b.reshape(-1, 1)
    tidx = jnp.arange(MOE_TILE)
    su = (tidx[:, None] < tidx[None, :]).astype(bf)

    xt = x.reshape(T, D)
    v_first = None
    pending = None
    for i in range(L):
        sh1, sc1, gt1, sh2, sc2, gt2 = [mod[i, :, j * D:(j + 1) * D] for j in range(6)]
        w_m = jnp.where(causal, w_s[i], 0.0)
        p = dict(
            w_rw=w_in[i, :, :rw_cols].astype(bf),
            w_sg=w_in[i, :, rw_cols:rw_cols + 2 * S].astype(bf),
            w_gt=w_in[i, :, rw_cols + 2 * S:].astype(bf),
            mu=rwkv_mu[i][None], w0=w0[i][None], a0=a0[i][None],
            w2p=jnp.concatenate([w2[i], jnp.zeros((al, W), f32)], 0).astype(bf),
            a2p=jnp.concatenate([jnp.zeros((dl, W), f32), a2[i]], 0).astype(bf),
            g2=g2[i].astype(bf), k_k=k_k[i][None], k_a=k_a[i][None], hsum=hsum,
            sg_ln_g=sg_ln_g[i][None], sg_ln_b=sg_ln_b[i][None],
            wpair=jnp.concatenate([w_m[0::2], w_m[1::2]], axis=2).astype(bf),
            sbias=jnp.repeat(b_s[i].T, HEAD_DIM, axis=1),
            w_bb=w_branch_b[i].astype(bf), w_ba=w_branch_a[i].astype(bf), w_out=w_out[i].astype(bf),
            r_k=r_k[i].reshape(1, W), gn_g=gn_g[i][None], gn_b=gn_b[i][None],
            ln1_g=ln1_g[i][None], ln1_b=ln1_b[i][None],
        )
        if i > 0:
            p.update(vm0=vm0[i - 1][None], vm1=vm1[i - 1].astype(bf), vm2=vm2[i - 1].astype(bf))
        outs = _front(xt, sh1, sc1, p, v_first, tm=512 if pending is None else 256, prev=pending)
        r, k, v, kk, b, lw, g, ga, mixb = outs[:9]
        if pending is not None:
            xt = outs[9]
        if i == 0:
            v_first = v
        y = _scan(r, k, v, kk, b, lw, tt=256)
        xt = _post(alpha, y, r, k, v, g, ga, mixb, xt, gt1, p, tm=256)
        h, wt, srow, scol, cnt = _route(xt, sh2, sc2, router_wt, router_bc, su, tm=MOE_TILE)
        yg = _moe(h, wt.T, srow, scol, cnt, e_gate[i].astype(bf), e_up[i].astype(bf), e_down[i].astype(bf),
                  tm=MOE_TILE)
        pending = (alpha, yg, gt2, ln2_g[i][None], ln2_b[i][None])
    xt = _combine(*pending[:1], xt, *pending[1:], tm=512)
    return xt.reshape(B, T, D)
```

```python
import functools

import jax
import jax.numpy as jnp
from jax import lax
from jax.experimental import pallas as pl
from jax.experimental.pallas import tpu as pltpu

LN_EPS = 1e-5
GN_EPS = 64e-5
N_EXPERT_GROUPS = 4
LANES = 128
SUBLANES = 8
MOE_BLOCK = 128
SCAN_CHUNK = 64
HEAD_DIM = 64
MOE_TILE = 1024
VMEM_LIMIT = 56 * 1024 * 1024
FRONT_VMEM_LIMIT = 60 * 1024 * 1024

_HI = lax.Precision.HIGHEST
_NT = (((1,), (1,)), ((), ()))
_TN = (((0,), (0,)), ((), ()))


def _bdot(a, b):
    return jnp.dot(a.astype(jnp.bfloat16), b.astype(jnp.bfloat16), preferred_element_type=jnp.float32)


def _split_dot(a, b_exact, terms=2):
    acc = None
    rem = a
    for _ in range(terms):
        piece = rem.astype(jnp.bfloat16)
        part = jnp.dot(piece, b_exact, preferred_element_type=jnp.float32)
        acc = part if acc is None else acc + part
        rem = rem - piece.astype(jnp.float32)
    return acc


def _sigmoid(x):
    return 1.0 / (1.0 + jnp.exp(-x))


def _softplus(x):
    return jnp.maximum(x, 0.0) + jnp.log(1.0 + jnp.exp(-jnp.abs(x)))


def _ln_rows(x, eps):
    mu = jnp.mean(x, axis=-1, keepdims=True)
    xc = x - mu
    var = jnp.mean(xc * xc, axis=-1, keepdims=True)
    return xc * lax.rsqrt(var + eps)


def _combine_rows(alpha, x, yg_ref, gate_ref, lng_ref, lnb_ref):
    y = yg_ref[0].astype(jnp.float32)
    for g in range(1, yg_ref.shape[0]):
        y = y + yg_ref[g].astype(jnp.float32)
    xn = alpha * x + (1.0 + gate_ref[...]) * y
    return _ln_rows(xn, LN_EPS) * lng_ref[...] + lnb_ref[...]


def _const_spec(shape):
    nd = len(shape)
    return pl.BlockSpec(shape, lambda *_: (0,) * nd, pipeline_mode=pl.Buffered(1))


def _mod_kernel(c_ref, w_ref, b_ref, o_ref):
    c = c_ref[...]
    cond = c * _sigmoid(c)
    o_ref[0] = jnp.sum(cond * w_ref[0], axis=0, keepdims=True) + b_ref[0]


def _modulation(c, w_mod, b_mod):
    L, D, D6 = w_mod.shape
    nb = D6 // D
    out = pl.pallas_call(
        _mod_kernel,
        grid=(L, nb),
        in_specs=[pl.BlockSpec((D, 1), lambda l, j: (0, 0)),
                  pl.BlockSpec((1, D, D), lambda l, j: (l, 0, j)),
                  pl.BlockSpec((1, 1, D), lambda l, j: (l, 0, j))],
        out_specs=pl.BlockSpec((1, 1, D), lambda l, j: (l, 0, j)),
        out_shape=jax.ShapeDtypeStruct((L, 1, D6), jnp.float32),
        compiler_params=pltpu.CompilerParams(dimension_semantics=("arbitrary", "arbitrary"),
                                             vmem_limit_bytes=VMEM_LIMIT),
        name="adaln_mod",
    )(c.reshape(D, 1), w_mod, b_mod.reshape(L, 1, D6))
    return out


def _front_kernel(has_vmix, prev_alpha, W, *refs):
    x_ref = refs[0]
    pos = 1
    if prev_alpha is not None:
        yg_ref, pgate_ref, plng_ref, plnb_ref = refs[pos:pos + 4]
        pos += 4
    (sh_ref, sc_ref, wrw_ref, wsg_ref, wgt_ref, mu_ref, w0_ref, w2p_ref, a0_ref, a2p_ref, g2_ref,
     kkw_ref, ka_ref, hsum_ref) = refs[pos:pos + 14]
    pos += 14
    if has_vmix:
        vf_ref, vm0_ref, vm1_ref, vm2_ref = refs[pos:pos + 4]
        pos += 4
    lng_ref, lnb_ref, wpair_ref, sbias_ref, wbb_ref = refs[pos:pos + 5]
    pos += 5
    r_o, k_o, v_o, kk_o, b_o, lw_o, g_o, ga_o, mixb_o = refs[pos:pos + 9]
    pos += 9
    if prev_alpha is not None:
        x_o = refs[pos]
        pos += 1
    carry_ref = refs[pos]

    tm = x_ref.shape[0]

    @pl.when(pl.program_id(0) == 0)
    def _():
        carry_ref[...] = jnp.zeros_like(carry_ref)

    x = x_ref[...]
    if prev_alpha is not None:
        x = _combine_rows(prev_alpha, x, yg_ref, pgate_ref, plng_ref, plnb_ref)
        x_o[...] = x
    h = _ln_rows(x, LN_EPS) * (1.0 + sc_ref[...]) + sh_ref[...]
    hb = h.astype(jnp.bfloat16)

    z = jnp.dot(hb, wrw_ref[...], preferred_element_type=jnp.float32)
    row = lax.broadcasted_iota(jnp.int32, z.shape, 0)
    prev = jnp.where(row == 0, carry_ref[...], pltpu.roll(z, 1, axis=0))
    carry_ref[...] = z[tm - 1:tm, :]
    z = z + (prev - z) * mu_ref[...]
    r = z[:, :W]
    k = z[:, W:2 * W]
    v = z[:, 2 * W:3 * W]
    zwa = z[:, 3 * W:3 * W + LANES]
    zg = z[:, 3 * W + LANES:]
    w_log = -_softplus(-(w0_ref[...] + _bdot(jnp.tanh(zwa), w2p_ref[...]))) - 0.5
    lw_o[...] = -jnp.exp(w_log)
    a = _sigmoid(a0_ref[...] + _bdot(zwa, a2p_ref[...]))
    g_o[...] = _bdot(_sigmoid(zg), g2_ref[...])
    if has_vmix:
        mix = _sigmoid(vm0_ref[...] + _bdot(_bdot(v, vm1_ref[...]), vm2_ref[...]))
        v = v + (vf_ref[...] - v) * mix
    kk = k * kkw_ref[...]
    ss = _split_dot(kk * kk, hsum_ref[...], terms=1)
    kk = kk / jnp.maximum(jnp.sqrt(ss), 1e-12)
    r_o[...] = r
    k_o[...] = k * (1.0 + (a - 1.0) * ka_ref[...])
    v_o[...] = v
    kk_o[...] = kk
    b_o[...] = kk * a

    zs = jnp.dot(hb, wsg_ref[...], preferred_element_type=jnp.float32)
    zs = zs * (0.5 * (1.0 + jnp.tanh(0.7978845608028654 * (zs + 0.044715 * (zs * zs * zs)))))
    S = zs.shape[1] // 2
    u = zs[:, :S]
    vln = _ln_rows(zs[:, S:], LN_EPS) * lng_ref[...] + lnb_ref[...]
    lo = lax.broadcasted_iota(jnp.int32, (LANES, LANES), 1) < HEAD_DIM
    chunks = []
    for c in range(tm // LANES):
        cols = []
        for p in range(S // LANES):
            vp = vln[c * LANES:(c + 1) * LANES, p * LANES:(p + 1) * LANES]
            vs = jnp.concatenate([jnp.where(lo, vp, 0.0), jnp.where(lo, 0.0, vp)], axis=0)
            cols.append(_bdot(wpair_ref[p], vs))
        chunks.append(jnp.concatenate(cols, axis=1) + sbias_ref[...])
    yb = u * jnp.concatenate(chunks, axis=0)

    zg2 = jnp.dot(hb, wgt_ref[...], preferred_element_type=jnp.float32)
    Dm = zg2.shape[1] // 2
    ga_o[...] = _sigmoid(zg2[:, :Dm])
    mixb_o[...] = _sigmoid(zg2[:, Dm:]) * _bdot(yb, wbb_ref[...])


def _front(x, shift, scale, p, v_first, tm, prev=None):
    T, D = x.shape
    W = p["hsum"].shape[0]
    has_vmix = v_first is not None
    row = lambda w: pl.BlockSpec((tm, w), lambda i: (i, 0))
    ins = [x]
    specs = [row(D)]
    if prev is not None:
        ins += list(prev[1:])
        specs += [pl.BlockSpec((prev[1].shape[0], tm, D), lambda i: (0, i, 0))] + [_const_spec(a.shape)
                                                                                   for a in prev[2:]]
    consts = [shift, scale, p["w_rw"], p["w_sg"], p["w_gt"], p["mu"], p["w0"], p["w2p"], p["a0"], p["a2p"], p["g2"],
              p["k_k"], p["k_a"], p["hsum"]]
    ins += consts
    specs += [_const_spec(a.shape) for a in consts]
    if has_vmix:
        extra = [v_first, p["vm0"], p["vm1"], p["vm2"]]
        ins += extra
        specs += [row(W)] + [_const_spec(a.shape) for a in extra[1:]]
    tail = [p["sg_ln_g"], p["sg_ln_b"], p["wpair"], p["sbias"], p["w_bb"]]
    ins += tail
    specs += [_const_spec(a.shape) for a in tail]
    f32 = jnp.float32
    n_wide = 2 if prev is None else 3
    out_shape = [jax.ShapeDtypeStruct((T, W), f32)] * 7 + [jax.ShapeDtypeStruct((T, D), f32)] * n_wide
    out_specs = [row(W)] * 7 + [row(D)] * n_wide
    return pl.pallas_call(
        functools.partial(_front_kernel, has_vmix, None if prev is None else prev[0], W),
        grid=(T // tm,),
        in_specs=specs,
        out_specs=out_specs,
        out_shape=out_shape,
        scratch_shapes=[pltpu.VMEM((1, p["w_rw"].shape[1]), f32)],
        compiler_params=pltpu.CompilerParams(dimension_semantics=("arbitrary",), vmem_limit_bytes=FRONT_VMEM_LIMIT),
        name="mixer_front",
    )(*ins)


def _scan_kernel(r_ref, k_ref, v_ref, kk_ref, b_ref, lw_ref, y_ref, s_ref):
    C = SCAN_CHUNK
    P = LANES
    bf = jnp.bfloat16
    tt, width = lw_ref.shape
    n_chunks = tt // C
    n_blocks = width // P

    @pl.when(pl.program_id(0) == 0)
    def _():
        s_ref[...] = jnp.zeros_like(s_ref)

    lw = lw_ref[...]
    ti = lax.broadcasted_iota(jnp.int32, (tt, tt), 0)
    tj = lax.broadcasted_iota(jnp.int32, (tt, tt), 1)
    same_chunk = (ti // C) == (tj // C)
    pre = jnp.concatenate([jnp.where(same_chunk & (ti >= tj), 1.0, 0.0),
                           jnp.where(same_chunk, 1.0, 0.0)], axis=0).astype(bf)
    sums = None
    rem = lw
    for _ in range(3):
        piece = rem.astype(bf)
        part = jnp.dot(pre, piece, preferred_element_type=jnp.float32)
        sums = part if sums is None else sums + part
        rem = rem - piece.astype(jnp.float32)
    cum, tot = sums[:tt], sums[tt:]
    w_inv = jnp.exp(-cum)
    w_end = jnp.exp(tot - cum)
    w_tot = jnp.exp(tot)

    kk = kk_ref[...]
    bb = b_ref[...]
    kx = k_ref[...]
    a_t = -kk * jnp.exp(cum - lw)
    b_t = (bb * w_inv).astype(bf)
    k_t = (kx * w_inv).astype(bf)
    r_t = r_ref[...] * jnp.exp(cum)
    b_h = (bb * w_end).astype(bf)
    k_h = (kx * w_end).astype(bf)
    vv = v_ref[...]

    ri = lax.broadcasted_iota(jnp.int32, (P, P), 0)
    ci = lax.broadcasted_iota(jnp.int32, (P, P), 1)
    same = (ri >= C) == (ci >= C)
    strict = same & (ri > ci)
    incl = same & (ri >= ci)
    eye = ri == ci
    lo = lax.broadcasted_iota(jnp.int32, (C, P), 1) < HEAD_DIM

    def stack(m):
        return jnp.concatenate([jnp.where(lo, m, 0.0), jnp.where(lo, 0.0, m)], axis=0)

    def fold(m):
        return m[:C] + m[C:]

    def mm(a, b, dims=(((1,), (0,)), ((), ()))):
        return lax.dot_general(a, b, dims, preferred_element_type=jnp.float32)

    chains = [(j, p) for j in range(n_chunks) for p in range(n_blocks)]
    n = len(chains)
    cut = lambda m, j, p: m[j * C:(j + 1) * C, p * P:(p + 1) * P]
    a_s = [stack(cut(a_t, j, p)).astype(bf) for j, p in chains]
    r_s = [stack(cut(r_t, j, p)) for j, p in chains]
    v_s = [stack(cut(vv, j, p)).astype(bf) for j, p in chains]
    q = [mm(jnp.concatenate([a_s[i], r_s[i].astype(bf)], axis=0),
            jnp.concatenate([cut(b_t, j, p)] * 2 + [cut(k_t, j, p)] * 2, axis=0), _NT)
         for i, (j, p) in enumerate(chains)]
    l_ab = [jnp.where(strict, m[:P, :P], 0.0) for m in q]
    low = [jnp.concatenate([jnp.where(strict, m[:P, P:], 0.0), jnp.where(incl, m[P:, P:], 0.0)], axis=0).astype(bf)
           for m in q]
    p_rb = [jnp.where(incl, m[P:, :P], 0.0).astype(bf) for m in q]
    inv = [jnp.where(eye, 1.0, jnp.where((ri >> 1) == (ci >> 1), m, 0.0)) for m in l_ab]
    for lg in range(1, 6):
        off = ((ri >> (lg + 1)) == (ci >> (lg + 1))) & (((ri >> lg) & 1) == 1) & (((ci >> lg) & 1) == 0)
        inv_b = [m.astype(bf) for m in inv]
        tmp = [mm(jnp.where(off, l_ab[i], 0.0).astype(bf), inv_b[i]) for i in range(n)]
        inv = [inv[i] + mm(inv_b[i], tmp[i].astype(bf)) for i in range(n)]
    lv = [mm(low[i], v_s[i]) for i in range(n)]
    ta = [mm(inv[i].astype(bf), jnp.concatenate([a_s[i], lv[i][:P].astype(bf)], axis=1))
          for i in range(n)]
    pa = [mm(p_rb[i], ta[i].astype(bf)) + jnp.concatenate([r_s[i], lv[i][P:]], axis=1) for i in range(n)]
    a_p = [fold(m[:, :P]) for m in ta]
    u_v = [fold(m[:, P:]) for m in ta]
    r_p = [fold(m[:, :P]).astype(bf) for m in pa]
    y_loc = [fold(m[:, P:]) for m in pa]
    zero = jnp.zeros((C, P), jnp.float32)
    gh = [mm(jnp.concatenate([cut(b_h, j, p), cut(k_h, j, p)], axis=0),
             jnp.concatenate([jnp.concatenate([a_p[i], u_v[i]], axis=1),
                              jnp.concatenate([zero, cut(vv, j, p)], axis=1)], axis=0).astype(bf), _TN)
          for i, (j, p) in enumerate(chains)]
    g_m = [(jnp.where(same, gh[i][:, :P], 0.0) + jnp.where(eye, cut(w_tot, j, p)[:1], 0.0)).astype(bf)
           for i, (j, p) in enumerate(chains)]
    h_m = [jnp.where(same, m[:, P:], 0.0) for m in gh]
    state = [s_ref[p] for p in range(n_blocks)]
    for i, (j, p) in enumerate(chains):
        s_b = state[p].astype(bf)
        y_ref[j * C:(j + 1) * C, p * P:(p + 1) * P] = mm(r_p[i], s_b) + y_loc[i]
        state[p] = mm(g_m[i], s_b) + h_m[i]
    for p in range(n_blocks):
        s_ref[p] = state[p]


def _scan(r, k, v, kk, b, lw, tt):
    T, W = r.shape
    spec = pl.BlockSpec((tt, W), lambda i: (i, 0))
    return pl.pallas_call(
        _scan_kernel,
        grid=(T // tt,),
        in_specs=[spec] * 6,
        out_specs=spec,
        out_shape=jax.ShapeDtypeStruct((T, W), jnp.float32),
        scratch_shapes=[pltpu.VMEM((W // LANES, LANES, LANES), jnp.float32)],
        compiler_params=pltpu.CompilerParams(dimension_semantics=("arbitrary",), vmem_limit_bytes=VMEM_LIMIT),
        name="wkv7_scan",
    )(r, k, v, kk, b, lw)


def _post_kernel(alpha, y_ref, r_ref, k_ref, v_ref, g_ref, ga_ref, mixb_ref, x_ref, rk_ref, gng_ref, gnb_ref,
                 hsum_ref, wba_ref, wout_ref, gate_ref, lng_ref, lnb_ref, o_ref):
    y = y_ref[...]
    hs = hsum_ref[...]
    inv_n = 1.0 / HEAD_DIM
    mu = _split_dot(y, hs) * inv_n
    yc = y - mu
    var = _split_dot(yc * yc, hs, terms=1) * inv_n
    yn = yc * lax.rsqrt(var + GN_EPS) * gng_ref[...] + gnb_ref[...]
    v = v_ref[...]
    bonus = _split_dot(r_ref[...] * k_ref[...] * rk_ref[...], hs, terms=1)
    ya = (yn + bonus * v) * g_ref[...]
    mixed = ga_ref[...] * _bdot(ya, wba_ref[...]) + mixb_ref[...]
    mixed = _bdot(mixed, wout_ref[...])
    xn = alpha * x_ref[...] + (1.0 + gate_ref[...]) * mixed
    o_ref[...] = _ln_rows(xn, LN_EPS) * lng_ref[...] + lnb_ref[...]


def _post(alpha, y, r, k, v, g, ga, mixb, x, gate, p, tm):
    T, D = x.shape
    W = y.shape[1]
    row = lambda w: pl.BlockSpec((tm, w), lambda i: (i, 0))
    consts = [p["r_k"], p["gn_g"], p["gn_b"], p["hsum"], p["w_ba"], p["w_out"], gate, p["ln1_g"], p["ln1_b"]]
    return pl.pallas_call(
        functools.partial(_post_kernel, alpha),
        grid=(T // tm,),
        in_specs=[row(W)] * 5 + [row(D)] * 3 + [_const_spec(a.shape) for a in consts],
        out_specs=row(D),
        out_shape=jax.ShapeDtypeStruct((T, D), jnp.float32),
        compiler_params=pltpu.CompilerParams(dimension_semantics=("arbitrary",), vmem_limit_bytes=VMEM_LIMIT),
        name="mixer_post",
    )(y, r, k, v, g, ga, mixb, x, *consts)


def _route_kernel(x_ref, sh_ref, sc_ref, rwt_ref, rb_ref, su_ref, h_ref, wt_ref, srow_ref, scol_ref, cnt_ref):
    h = _ln_rows(x_ref[...], LN_EPS) * (1.0 + sc_ref[...]) + sh_ref[...]
    h_ref[...] = h.astype(jnp.bfloat16)
    logits = lax.dot_general(rwt_ref[...], h, _NT, precision=_HI, preferred_element_type=jnp.float32)
    scores = _sigmoid(logits)
    sel = scores + rb_ref[...]
    E = sel.shape[0]
    per = E // N_EXPERT_GROUPS
    rows = [sel[e:e + 1, :] for e in range(E)]
    gscore = []
    for g in range(N_EXPERT_GROUPS):
        m = rows[g * per:(g + 1) * per]
        best = None
        for i in range(per):
            for j in range(i + 1, per):
                s2 = m[i] + m[j]
                best = s2 if best is None else jnp.maximum(best, s2)
        gscore.append(best)
    cur = gscore[0]
    bestg = jnp.zeros(cur.shape, jnp.int32)
    for g in range(1, N_EXPERT_GROUPS):
        take = gscore[g] > cur
        cur = jnp.where(take, gscore[g], cur)
        bestg = jnp.where(take, g, bestg)
    picked = []
    for e in range(E):
        g = e // per
        rank = jnp.zeros(cur.shape, jnp.int32)
        for j in range(g * per, (g + 1) * per):
            if j == e:
                continue
            ahead = (rows[j] > rows[e]) | ((rows[j] == rows[e]) & (j < e))
            rank = rank + ahead.astype(jnp.int32)
        picked.append(jnp.where((bestg == g) & (rank < 2), scores[e:e + 1, :], 0.0))
    total = picked[0]
    for e in range(1, E):
        total = total + picked[e]
    wt_ref[...] = jnp.concatenate(picked, axis=0) / total
    gsel = jnp.concatenate([jnp.where(bestg == g, 1.0, 0.0) for g in range(N_EXPERT_GROUPS)]
                           + [jnp.zeros_like(cur)] * (2 * SUBLANES - N_EXPERT_GROUPS), axis=0)
    pos = jnp.dot(gsel.astype(jnp.bfloat16), su_ref[...], preferred_element_type=jnp.float32)[:SUBLANES]
    gsel = gsel[:SUBLANES]
    cnt = jnp.sum(gsel, axis=-1, keepdims=True)
    cnt_ref[0] = jnp.broadcast_to(cnt, (SUBLANES, LANES))
    padded = jnp.ceil(cnt * (1.0 / MOE_BLOCK)) * MOE_BLOCK
    rowi = lax.broadcasted_iota(jnp.int32, (SUBLANES, 1), 0)
    offs = jnp.zeros_like(cnt)
    run = jnp.zeros((1, 1), jnp.float32)
    for g in range(1, N_EXPERT_GROUPS):
        run = run + padded[g - 1:g, :]
        offs = jnp.where(rowi == g, run, offs)
    slot = jnp.sum(gsel * (pos + offs), axis=0, keepdims=True)
    srow_ref[...] = slot
    scol_ref[...] = jnp.transpose(jnp.broadcast_to(slot, (LANES, slot.shape[1])))


def _route(x, shift, scale, router_wt, router_b, su, tm):
    T, D = x.shape
    E = router_wt.shape[0]
    f32 = jnp.float32
    return pl.pallas_call(
        _route_kernel,
        grid=(T // tm,),
        in_specs=[pl.BlockSpec((tm, D), lambda i: (i, 0)), _const_spec(shift.shape), _const_spec(scale.shape),
                  _const_spec(router_wt.shape), _const_spec(router_b.shape), _const_spec(su.shape)],
        out_specs=[pl.BlockSpec((tm, D), lambda i: (i, 0)), pl.BlockSpec((E, tm), lambda i: (0, i)),
                   pl.BlockSpec((1, tm), lambda i: (0, i)), pl.BlockSpec((tm, LANES), lambda i: (i, 0)),
                   pl.BlockSpec((1, SUBLANES, LANES), lambda i: (i, 0, 0))],
        out_shape=[jax.ShapeDtypeStruct((T, D), jnp.bfloat16), jax.ShapeDtypeStruct((E, T), f32),
                   jax.ShapeDtypeStruct((1, T), f32), jax.ShapeDtypeStruct((T, LANES), f32),
                   jax.ShapeDtypeStruct((T // tm, SUBLANES, LANES), f32)],
        compiler_params=pltpu.CompilerParams(dimension_semantics=("arbitrary",), vmem_limit_bytes=VMEM_LIMIT),
        name="moe_route",
    )(x, shift, scale, router_wt, router_b, su)


def _moe_kernel(nblk_ref, off_ref, h_ref, wt_ref, srow_ref, scol_ref, eg_ref, eu_ref, ed_ref, o_ref, acc_ref,
                stage_ref):
    g = pl.program_id(0)
    i = pl.program_id(1)
    G = pl.num_programs(0)
    tm = h_ref.shape[0]
    per, _, F = eg_ref.shape
    bf, f32 = jnp.bfloat16, jnp.float32
    dot = functools.partial(jnp.dot, preferred_element_type=f32)
    acc_ref[...] = jnp.zeros_like(acc_ref)
    wt = wt_ref[...]
    w_hi = wt.astype(bf)
    w_r = wt - w_hi.astype(f32)
    w_mid = w_r.astype(bf)
    w_lo = (w_r - w_mid.astype(f32)).astype(bf)
    w_parts = jnp.concatenate([w_hi, w_mid, w_lo], axis=1)
    E = wt.shape[1]
    first = off_ref[i * G + g].astype(f32)

    def experts(base):
        srow = lax.broadcasted_iota(jnp.int32, (MOE_BLOCK, tm), 0).astype(f32) + base
        sel = jnp.where(srow == srow_ref[...], 1.0, 0.0).astype(bf)
        xb = dot(sel, h_ref[...]).astype(bf)
        wsp = dot(sel, w_parts)
        wsb = wsp[:, :E] + wsp[:, E:2 * E] + wsp[:, 2 * E:]
        lane = lax.broadcasted_iota(jnp.int32, wsb.shape, 1)
        out = None
        for e in range(per):
            wcol = jnp.sum(jnp.where(lane == g * per + e, wsb, 0.0), axis=-1, keepdims=True)
            gt = dot(xb, eg_ref[e])
            hid = gt * _sigmoid(gt) * dot(xb, eu_ref[e]) * wcol
            part = dot(hid.astype(bf), ed_ref[e])
            out = part if out is None else out + part
        return out.astype(bf)

    n = nblk_ref[i * G + g]

    def pair(jp, carry):
        base = first + (jp * (2 * MOE_BLOCK)).astype(f32)
        stage_ref[:MOE_BLOCK, :] = experts(base)
        second = 2 * jp + 1 < n

        @pl.when(second)
        def _():
            stage_ref[MOE_BLOCK:, :] = experts(base + float(MOE_BLOCK))

        @pl.when(jnp.logical_not(second))
        def _():
            stage_ref[MOE_BLOCK:, :] = jnp.zeros((MOE_BLOCK, stage_ref.shape[1]), bf)

        scol = lax.broadcasted_iota(jnp.int32, (tm, 2 * LANES), 1).astype(f32) + base
        mine = jnp.concatenate([scol_ref[...]] * 2, axis=1)
        back = jnp.where(scol == mine, 1.0, 0.0).astype(bf)
        acc_ref[...] += dot(back, stage_ref[...])
        return carry

    lax.fori_loop(0, (n + 1) // 2, pair, 0)
    o_ref[0] = acc_ref[...].astype(bf)


def _combine_kernel(alpha, x_ref, y_ref, gate_ref, lng_ref, lnb_ref, o_ref):
    o_ref[...] = _combine_rows(alpha, x_ref[...], y_ref, gate_ref, lng_ref, lnb_ref)


def _moe(h, wt, srow, scol, cnt, eg, eu, ed, tm):
    T, D = h.shape
    E, _, F = eg.shape
    G = N_EXPERT_GROUPS
    per = E // G
    nt = T // tm
    cnt_i = cnt[:, :G, 0].astype(jnp.int32)
    nblk = (cnt_i + MOE_BLOCK - 1) // MOE_BLOCK
    off = (jnp.cumsum(nblk, axis=1) - nblk) * MOE_BLOCK
    tok = lambda w: pl.BlockSpec((tm, w), lambda g, i, *_: (i, 0))
    grp = lambda a, b: pl.BlockSpec((per, a, b), lambda g, i, *_: (g, 0, 0), pipeline_mode=pl.Buffered(1))
    return pl.pallas_call(
        _moe_kernel,
        grid_spec=pltpu.PrefetchScalarGridSpec(
            num_scalar_prefetch=2,
            grid=(G, nt),
            in_specs=[tok(D), tok(E), pl.BlockSpec((1, tm), lambda g, i, *_: (0, i)), tok(LANES),
                      grp(D, F), grp(D, F), grp(F, D)],
            out_specs=pl.BlockSpec((1, tm, D), lambda g, i, *_: (g, i, 0)),
            scratch_shapes=[pltpu.VMEM((tm, D), jnp.float32), pltpu.VMEM((2 * MOE_BLOCK, D), jnp.bfloat16)]),
        out_shape=jax.ShapeDtypeStruct((G, T, D), jnp.bfloat16),
        compiler_params=pltpu.CompilerParams(dimension_semantics=("arbitrary", "arbitrary"),
                                             vmem_limit_bytes=VMEM_LIMIT),
        name="moe_experts",
    )(nblk.reshape(-1), off.reshape(-1), h, wt, srow, scol, eg, eu, ed)


def _combine(alpha, x, yg, gate, ln_g, ln_b, tm):
    T, D = x.shape
    vec = pl.BlockSpec((1, D), lambda i: (0, 0))
    return pl.pallas_call(
        functools.partial(_combine_kernel, alpha),
        grid=(T // tm,),
        in_specs=[pl.BlockSpec((tm, D), lambda i: (i, 0)), pl.BlockSpec((yg.shape[0], tm, D), lambda i: (0, i, 0)),
                  vec, vec, vec],
        out_specs=pl.BlockSpec((tm, D), lambda i: (i, 0)),
        out_shape=jax.ShapeDtypeStruct((T, D), jnp.float32),
        compiler_params=pltpu.CompilerParams(dimension_semantics=("arbitrary",), vmem_limit_bytes=VMEM_LIMIT),
        name="moe_combine",
    )(x, yg, gate, ln_g, ln_b)


def kernel(x, c, w_mod, b_mod, w_in, rwkv_mu, w0, w2, a0, a2, g2, k_k, k_a, r_k, gn_g, gn_b, vm0, vm1, vm2,
           sg_ln_g, sg_ln_b, w_s, b_s, w_branch_a, w_branch_b, w_out, ln1_g, ln1_b, router_w, router_b,
           e_gate, e_up, e_down, ln2_g, ln2_b):
    B, T, D = x.shape
    assert B == 1
    L = w_mod.shape[0]
    W = w0.shape[1]
    S = sg_ln_g.shape[1]
    dl, al = w2.shape[1], a2.shape[1]
    assert dl + al == LANES and W % LANES == 0 and S % LANES == 0
    rw_cols = rwkv_mu.shape[1]
    G, CH = w_s.shape[1], w_s.shape[2]
    assert CH == LANES and S // G == HEAD_DIM and W // r_k.shape[1] == HEAD_DIM
    alpha = float((2 * L) ** 0.25)
    bf = jnp.bfloat16
    f32 = jnp.float32

    mod = _modulation(c, w_mod, b_mod)
    head = jnp.arange(W) // HEAD_DIM
    hsum = (head[:, None] == head[None, :]).astype(bf)
    causal = jnp.tril(jnp.ones((CH, CH), bool))
    router_wt = router_w.T
    router_bc = router_b.reshape(-1, 1)
    tidx = jnp.arange(MOE_TILE)
    su = (tidx[:, None] < tidx[None, :]).astype(bf)

    xt = x.reshape(T, D)
    v_first = None
    pending = None
    for i in range(L):
        sh1, sc1, gt1, sh2, sc2, gt2 = [mod[i, :, j * D:(j + 1) * D] for j in range(6)]
        w_m = jnp.where(causal, w_s[i], 0.0)
        p = dict(
            w_rw=w_in[i, :, :rw_cols].astype(bf),
            w_sg=w_in[i, :, rw_cols:rw_cols + 2 * S].astype(bf),
            w_gt=w_in[i, :, rw_cols + 2 * S:].astype(bf),
            mu=rwkv_mu[i][None], w0=w0[i][None], a0=a0[i][None],
            w2p=jnp.concatenate([w2[i], jnp.zeros((al, W), f32)], 0).astype(bf),
            a2p=jnp.concatenate([jnp.zeros((dl, W), f32), a2[i]], 0).astype(bf),
            g2=g2[i].astype(bf), k_k=k_k[i][None], k_a=k_a[i][None], hsum=hsum,
            sg_ln_g=sg_ln_g[i][None], sg_ln_b=sg_ln_b[i][None],
            wpair=jnp.concatenate([w_m[0::2], w_m[1::2]], axis=2).astype(bf),
            sbias=jnp.repeat(b_s[i].T, HEAD_DIM, axis=1),
            w_bb=w_branch_b[i].astype(bf), w_ba=w_branch_a[i].astype(bf), w_out=w_out[i].astype(bf),
            r_k=r_k[i].reshape(1, W), gn_g=gn_g[i][None], gn_b=gn_b[i][None],
            ln1_g=ln1_g[i][None], ln1_b=ln1_b[i][None],
        )
        if i > 0:
            p.update(vm0=vm0[i - 1][None], vm1=vm1[i - 1].astype(bf), vm2=vm2[i - 1].astype(bf))
        outs = _front(xt, sh1, sc1, p, v_first, tm=512, prev=pending)
        r, k, v, kk, b, lw, g, ga, mixb = outs[:9]
        if pending is not None:
            xt = outs[9]
        if i == 0:
            v_first = v
        y = _scan(r, k, v, kk, b, lw, tt=256)
        xt = _post(alpha, y, r, k, v, g, ga, mixb, xt, gt1, p, tm=512)
        h, wt, srow, scol, cnt = _route(xt, sh2, sc2, router_wt, router_bc, su, tm=MOE_TILE)
        yg = _moe(h, wt.T, srow, scol, cnt, e_gate[i].astype(bf), e_up[i].astype(bf), e_down[i].astype(bf),
                  tm=MOE_TILE)
        pending = (alpha, yg, gt2, ln2_g[i][None], ln2_b[i][None])
    xt = _combine(*pending[:1], xt, *pending[1:], tm=512)
    return xt.reshape(B, T, D)
```

```python
import functools

import jax
import jax.numpy as jnp
from jax import lax
from jax.experimental import pallas as pl
from jax.experimental.pallas import tpu as pltpu

LN_EPS = 1e-5
GN_EPS = 64e-5
N_EXPERT_GROUPS = 4
LANES = 128
SUBLANES = 8
MOE_BLOCK = 128
SCAN_CHUNK = 64
HEAD_DIM = 64
MOE_TILE = 1024
VMEM_LIMIT = 56 * 1024 * 1024
FRONT_VMEM_LIMIT = 60 * 1024 * 1024

_HI = lax.Precision.HIGHEST
_NT = (((1,), (1,)), ((), ()))
_TN = (((0,), (0,)), ((), ()))


def _bdot(a, b):
    return jnp.dot(a.astype(jnp.bfloat16), b.astype(jnp.bfloat16), preferred_element_type=jnp.float32)


def _split_dot(a, b_exact, terms=2):
    acc = None
    rem = a
    for _ in range(terms):
        piece = rem.astype(jnp.bfloat16)
        part = jnp.dot(piece, b_exact, preferred_element_type=jnp.float32)
        acc = part if acc is None else acc + part
        rem = rem - piece.astype(jnp.float32)
    return acc


def _sigmoid(x):
    return 1.0 / (1.0 + jnp.exp(-x))


def _softplus(x):
    return jnp.maximum(x, 0.0) + jnp.log(1.0 + jnp.exp(-jnp.abs(x)))


def _ln_rows(x, eps):
    mu = jnp.mean(x, axis=-1, keepdims=True)
    xc = x - mu
    var = jnp.mean(xc * xc, axis=-1, keepdims=True)
    return xc * lax.rsqrt(var + eps)


def _combine_rows(alpha, x, yg_ref, gate_ref, lng_ref, lnb_ref):
    y = yg_ref[0].astype(jnp.float32)
    for g in range(1, yg_ref.shape[0]):
        y = y + yg_ref[g].astype(jnp.float32)
    xn = alpha * x + (1.0 + gate_ref[...]) * y
    return _ln_rows(xn, LN_EPS) * lng_ref[...] + lnb_ref[...]


def _const_spec(shape):
    nd = len(shape)
    return pl.BlockSpec(shape, lambda *_: (0,) * nd, pipeline_mode=pl.Buffered(1))


def _mod_kernel(c_ref, w_ref, b_ref, o_ref):
    c = c_ref[...]
    cond = c * _sigmoid(c)
    o_ref[0] = jnp.sum(cond * w_ref[0], axis=0, keepdims=True) + b_ref[0]


def _modulation(c, w_mod, b_mod):
    L, D, D6 = w_mod.shape
    nb = D6 // D
    out = pl.pallas_call(
        _mod_kernel,
        grid=(L, nb),
        in_specs=[pl.BlockSpec((D, 1), lambda l, j: (0, 0)),
                  pl.BlockSpec((1, D, D), lambda l, j: (l, 0, j)),
                  pl.BlockSpec((1, 1, D), lambda l, j: (l, 0, j))],
        out_specs=pl.BlockSpec((1, 1, D), lambda l, j: (l, 0, j)),
        out_shape=jax.ShapeDtypeStruct((L, 1, D6), jnp.float32),
        compiler_params=pltpu.CompilerParams(dimension_semantics=("arbitrary", "arbitrary"),
                                             vmem_limit_bytes=VMEM_LIMIT),
        name="adaln_mod",
    )(c.reshape(D, 1), w_mod, b_mod.reshape(L, 1, D6))
    return out


def _front_kernel(has_vmix, prev_alpha, W, *refs):
    x_ref = refs[0]
    pos = 1
    if prev_alpha is not None:
        yg_ref, pgate_ref, plng_ref, plnb_ref = refs[pos:pos + 4]
        pos += 4
    (sh_ref, sc_ref, win_ref, mu_ref, w0_ref, w2p_ref, a0_ref, a2p_ref, g2_ref,
     kkw_ref, ka_ref, hsum_ref) = refs[pos:pos + 12]
    pos += 12
    if has_vmix:
        vf_ref, vm0_ref, vm1_ref, vm2_ref = refs[pos:pos + 4]
        pos += 4
    lng_ref, lnb_ref, wpair_ref, sbias_ref, wbb_ref = refs[pos:pos + 5]
    pos += 5
    r_o, k_o, v_o, kk_o, b_o, lw_o, g_o, ga_o, mixb_o = refs[pos:pos + 9]
    pos += 9
    rw_cols = mu_ref.shape[1]
    sg_cols = 2 * lng_ref.shape[1]
    wrw_ref = win_ref.at[0, :, :rw_cols]
    wsg_ref = win_ref.at[0, :, rw_cols:rw_cols + sg_cols]
    wgt_ref = win_ref.at[0, :, rw_cols + sg_cols:]
    if prev_alpha is not None:
        x_o = refs[pos]
        pos += 1
    carry_ref = refs[pos]

    tm = x_ref.shape[0]

    @pl.when(pl.program_id(0) == 0)
    def _():
        carry_ref[...] = jnp.zeros_like(carry_ref)

    x = x_ref[...]
    if prev_alpha is not None:
        x = _combine_rows(prev_alpha, x, yg_ref, pgate_ref, plng_ref, plnb_ref)
        x_o[...] = x
    h = _ln_rows(x, LN_EPS) * (1.0 + sc_ref[...]) + sh_ref[...]
    hb = h.astype(jnp.bfloat16)

    z = jnp.dot(hb, wrw_ref[...], preferred_element_type=jnp.float32)
    row = lax.broadcasted_iota(jnp.int32, z.shape, 0)
    prev = jnp.where(row == 0, carry_ref[...], pltpu.roll(z, 1, axis=0))
    carry_ref[...] = z[tm - 1:tm, :]
    z = z + (prev - z) * mu_ref[...]
    r = z[:, :W]
    k = z[:, W:2 * W]
    v = z[:, 2 * W:3 * W]
    zwa = z[:, 3 * W:3 * W + LANES]
    zg = z[:, 3 * W + LANES:]
    w_log = -_softplus(-(w0_ref[...] + _bdot(jnp.tanh(zwa), w2p_ref[...]))) - 0.5
    lw_o[...] = -jnp.exp(w_log)
    a = _sigmoid(a0_ref[...] + _bdot(zwa, a2p_ref[...]))
    g_o[...] = _bdot(_sigmoid(zg), g2_ref[...])
    if has_vmix:
        mix = _sigmoid(vm0_ref[...] + _bdot(_bdot(v, vm1_ref[...]), vm2_ref[...]))
        v = v + (vf_ref[...] - v) * mix
    kk = k * kkw_ref[...]
    ss = _split_dot(kk * kk, hsum_ref[...], terms=1)
    kk = kk / jnp.maximum(jnp.sqrt(ss), 1e-12)
    r_o[...] = r
    k_o[...] = k * (1.0 + (a - 1.0) * ka_ref[...])
    v_o[...] = v
    kk_o[...] = kk
    b_o[...] = kk * a

    zs = jnp.dot(hb, wsg_ref[...], preferred_element_type=jnp.float32)
    zs = zs * (0.5 * (1.0 + jnp.tanh(0.7978845608028654 * (zs + 0.044715 * (zs * zs * zs)))))
    S = zs.shape[1] // 2
    u = zs[:, :S]
    vln = _ln_rows(zs[:, S:], LN_EPS) * lng_ref[...] + lnb_ref[...]
    lo = lax.broadcasted_iota(jnp.int32, (LANES, LANES), 1) < HEAD_DIM
    chunks = []
    for c in range(tm // LANES):
        cols = []
        for p in range(S // LANES):
            vp = vln[c * LANES:(c + 1) * LANES, p * LANES:(p + 1) * LANES]
            vs = jnp.concatenate([jnp.where(lo, vp, 0.0), jnp.where(lo, 0.0, vp)], axis=0)
            cols.append(_bdot(wpair_ref[p], vs))
        chunks.append(jnp.concatenate(cols, axis=1) + sbias_ref[...])
    yb = u * jnp.concatenate(chunks, axis=0)

    zg2 = jnp.dot(hb, wgt_ref[...], preferred_element_type=jnp.float32)
    Dm = zg2.shape[1] // 2
    ga_o[...] = _sigmoid(zg2[:, :Dm])
    mixb_o[...] = _sigmoid(zg2[:, Dm:]) * _bdot(yb, wbb_ref[...])


def _front(x, shift, scale, p, v_first, tm, prev=None):
    T, D = x.shape
    W = p["hsum"].shape[0]
    has_vmix = v_first is not None
    row = lambda w: pl.BlockSpec((tm, w), lambda i: (i, 0))
    ins = [x]
    specs = [row(D)]
    if prev is not None:
        ins += list(prev[1:])
        specs += [pl.BlockSpec((prev[1].shape[0], tm, D), lambda i: (0, i, 0))] + [_const_spec(a.shape)
                                                                                   for a in prev[2:]]
    w_in, layer = p["w_in"]
    consts = [shift, scale, w_in, p["mu"], p["w0"], p["w2p"], p["a0"], p["a2p"], p["g2"],
              p["k_k"], p["k_a"], p["hsum"]]
    specs += [_const_spec(a.shape) for a in consts]
    specs[len(ins) + 2] = pl.BlockSpec((1,) + w_in.shape[1:], lambda i: (layer, 0, 0),
                                       pipeline_mode=pl.Buffered(1))
    ins += consts
    if has_vmix:
        extra = [v_first, p["vm0"], p["vm1"], p["vm2"]]
        ins += extra
        specs += [row(W)] + [_const_spec(a.shape) for a in extra[1:]]
    tail = [p["sg_ln_g"], p["sg_ln_b"], p["wpair"], p["sbias"], p["w_bb"]]
    ins += tail
    specs += [_const_spec(a.shape) for a in tail]
    f32 = jnp.float32
    n_wide = 2 if prev is None else 3
    out_shape = [jax.ShapeDtypeStruct((T, W), f32)] * 7 + [jax.ShapeDtypeStruct((T, D), f32)] * n_wide
    out_specs = [row(W)] * 7 + [row(D)] * n_wide
    return pl.pallas_call(
        functools.partial(_front_kernel, has_vmix, None if prev is None else prev[0], W),
        grid=(T // tm,),
        in_specs=specs,
        out_specs=out_specs,
        out_shape=out_shape,
        scratch_shapes=[pltpu.VMEM((1, p["mu"].shape[1]), f32)],
        compiler_params=pltpu.CompilerParams(dimension_semantics=("arbitrary",), vmem_limit_bytes=FRONT_VMEM_LIMIT),
        name="mixer_front",
    )(*ins)


def _scan_kernel(r_ref, k_ref, v_ref, kk_ref, b_ref, lw_ref, y_ref, s_ref):
    C = SCAN_CHUNK
    P = LANES
    bf = jnp.bfloat16
    tt, width = lw_ref.shape
    n_chunks = tt // C
    n_blocks = width // P

    @pl.when(pl.program_id(0) == 0)
    def _():
        s_ref[...] = jnp.zeros_like(s_ref)

    lw = lw_ref[...]
    ti = lax.broadcasted_iota(jnp.int32, (tt, tt), 0)
    tj = lax.broadcasted_iota(jnp.int32, (tt, tt), 1)
    same_chunk = (ti // C) == (tj // C)
    pre = jnp.concatenate([jnp.where(same_chunk & (ti >= tj), 1.0, 0.0),
                           jnp.where(same_chunk, 1.0, 0.0)], axis=0).astype(bf)
    sums = None
    rem = lw
    for _ in range(3):
        piece = rem.astype(bf)
        part = jnp.dot(pre, piece, preferred_element_type=jnp.float32)
        sums = part if sums is None else sums + part
        rem = rem - piece.astype(jnp.float32)
    cum, tot = sums[:tt], sums[tt:]
    w_inv = jnp.exp(-cum)
    w_end = jnp.exp(tot - cum)
    w_tot = jnp.exp(tot)

    kk = kk_ref[...]
    bb = b_ref[...]
    kx = k_ref[...]
    a_t = -kk * jnp.exp(cum - lw)
    b_t = (bb * w_inv).astype(bf)
    k_t = (kx * w_inv).astype(bf)
    r_t = r_ref[...] * jnp.exp(cum)
    b_h = (bb * w_end).astype(bf)
    k_h = (kx * w_end).astype(bf)
    vv = v_ref[...]

    ri = lax.broadcasted_iota(jnp.int32, (P, P), 0)
    ci = lax.broadcasted_iota(jnp.int32, (P, P), 1)
    same = (ri >= C) == (ci >= C)
    strict = same & (ri > ci)
    incl = same & (ri >= ci)
    eye = ri == ci
    lo = lax.broadcasted_iota(jnp.int32, (C, P), 1) < HEAD_DIM

    def stack(m):
        return jnp.concatenate([jnp.where(lo, m, 0.0), jnp.where(lo, 0.0, m)], axis=0)

    def fold(m):
        return m[:C] + m[C:]

    def mm(a, b, dims=(((1,), (0,)), ((), ()))):
        return lax.dot_general(a, b, dims, preferred_element_type=jnp.float32)

    chains = [(j, p) for j in range(n_chunks) for p in range(n_blocks)]
    n = len(chains)
    cut = lambda m, j, p: m[j * C:(j + 1) * C, p * P:(p + 1) * P]
    a_s = [stack(cut(a_t, j, p)).astype(bf) for j, p in chains]
    r_s = [stack(cut(r_t, j, p)) for j, p in chains]
    v_s = [stack(cut(vv, j, p)).astype(bf) for j, p in chains]
    q = [mm(jnp.concatenate([a_s[i], r_s[i].astype(bf)], axis=0),
            jnp.concatenate([cut(b_t, j, p)] * 2 + [cut(k_t, j, p)] * 2, axis=0), _NT)
         for i, (j, p) in enumerate(chains)]
    l_ab = [jnp.where(strict, m[:P, :P], 0.0) for m in q]
    low = [jnp.concatenate([jnp.where(strict, m[:P, P:], 0.0), jnp.where(incl, m[P:, P:], 0.0)], axis=0).astype(bf)
           for m in q]
    p_rb = [jnp.where(incl, m[P:, :P], 0.0).astype(bf) for m in q]
    inv = [jnp.where(eye, 1.0, jnp.where((ri >> 1) == (ci >> 1), m, 0.0)) for m in l_ab]
    for lg in range(1, 6):
        off = ((ri >> (lg + 1)) == (ci >> (lg + 1))) & (((ri >> lg) & 1) == 1) & (((ci >> lg) & 1) == 0)
        inv_b = [m.astype(bf) for m in inv]
        tmp = [mm(jnp.where(off, l_ab[i], 0.0).astype(bf), inv_b[i]) for i in range(n)]
        inv = [inv[i] + mm(inv_b[i], tmp[i].astype(bf)) for i in range(n)]
    lv = [mm(low[i], v_s[i]) for i in range(n)]
    ta = [mm(inv[i].astype(bf), jnp.concatenate([a_s[i], lv[i][:P].astype(bf)], axis=1))
          for i in range(n)]
    pa = [mm(p_rb[i], ta[i].astype(bf)) + jnp.concatenate([r_s[i], lv[i][P:]], axis=1) for i in range(n)]
    a_p = [fold(m[:, :P]) for m in ta]
    u_v = [fold(m[:, P:]) for m in ta]
    r_p = [fold(m[:, :P]).astype(bf) for m in pa]
    y_loc = [fold(m[:, P:]) for m in pa]
    zero = jnp.zeros((C, P), jnp.float32)
    gh = [mm(jnp.concatenate([cut(b_h, j, p), cut(k_h, j, p)], axis=0),
             jnp.concatenate([jnp.concatenate([a_p[i], u_v[i]], axis=1),
                              jnp.concatenate([zero, cut(vv, j, p)], axis=1)], axis=0).astype(bf), _TN)
          for i, (j, p) in enumerate(chains)]
    g_m = [(jnp.where(same, gh[i][:, :P], 0.0) + jnp.where(eye, cut(w_tot, j, p)[:1], 0.0)).astype(bf)
           for i, (j, p) in enumerate(chains)]
    h_m = [jnp.where(same, m[:, P:], 0.0) for m in gh]
    state = [s_ref[p] for p in range(n_blocks)]
    for i, (j, p) in enumerate(chains):
        s_b = state[p].astype(bf)
        y_ref[j * C:(j + 1) * C, p * P:(p + 1) * P] = mm(r_p[i], s_b) + y_loc[i]
        state[p] = mm(g_m[i], s_b) + h_m[i]
    for p in range(n_blocks):
        s_ref[p] = state[p]


def _scan(r, k, v, kk, b, lw, tt):
    T, W = r.shape
    spec = pl.BlockSpec((tt, W), lambda i: (i, 0))
    return pl.pallas_call(
        _scan_kernel,
        grid=(T // tt,),
        in_specs=[spec] * 6,
        out_specs=spec,
        out_shape=jax.ShapeDtypeStruct((T, W), jnp.float32),
        scratch_shapes=[pltpu.VMEM((W // LANES, LANES, LANES), jnp.float32)],
        compiler_params=pltpu.CompilerParams(dimension_semantics=("arbitrary",), vmem_limit_bytes=VMEM_LIMIT),
        name="wkv7_scan",
    )(r, k, v, kk, b, lw)


def _post_kernel(alpha, y_ref, r_ref, k_ref, v_ref, g_ref, ga_ref, mixb_ref, x_ref, rk_ref, gng_ref, gnb_ref,
                 hsum_ref, wba_ref, wout_ref, gate_ref, lng_ref, lnb_ref, sh2_ref, sc2_ref, rw2_ref, rwh_ref, rb_ref,
                 o_ref, h_ref, wt_ref, gsel_ref):
    y = y_ref[...]
    hs = hsum_ref[...]
    inv_n = 1.0 / HEAD_DIM
    mu = _split_dot(y, hs) * inv_n
    yc = y - mu
    var = _split_dot(yc * yc, hs, terms=1) * inv_n
    yn = yc * lax.rsqrt(var + GN_EPS) * gng_ref[...] + gnb_ref[...]
    v = v_ref[...]
    bonus = _split_dot(r_ref[...] * k_ref[...] * rk_ref[...], hs, terms=1)
    ya = (yn + bonus * v) * g_ref[...]
    mixed = ga_ref[...] * _bdot(ya, wba_ref[...]) + mixb_ref[...]
    mixed = _bdot(mixed, wout_ref[...])
    xn = alpha * x_ref[...] + (1.0 + gate_ref[...]) * mixed
    x_new = _ln_rows(xn, LN_EPS) * lng_ref[...] + lnb_ref[...]
    o_ref[...] = x_new
    h = _ln_rows(x_new, LN_EPS) * (1.0 + sc2_ref[...]) + sh2_ref[...]
    h_ref[...] = h.astype(jnp.bfloat16)
    wt, gsel = _route_rows(h, rw2_ref, rwh_ref, rb_ref)
    wt_ref[...] = wt
    gsel_ref[...] = gsel.astype(jnp.bfloat16)


def _post(alpha, y, r, k, v, g, ga, mixb, x, gate, p, shift2, scale2, rw2, rwh, router_b, tm):
    T, D = x.shape
    W = y.shape[1]
    E = router_b.shape[0]
    row = lambda w: pl.BlockSpec((tm, w), lambda i: (i, 0))
    col = lambda n: pl.BlockSpec((n, tm), lambda i: (0, i))
    consts = [p["r_k"], p["gn_g"], p["gn_b"], p["hsum"], p["w_ba"], p["w_out"], gate, p["ln1_g"], p["ln1_b"],
              shift2, scale2, rw2, rwh, router_b]
    return pl.pallas_call(
        functools.partial(_post_kernel, alpha),
        grid=(T // tm,),
        in_specs=[row(W)] * 5 + [row(D)] * 3 + [_const_spec(a.shape) for a in consts],
        out_specs=[row(D), row(D), col(E), col(2 * SUBLANES)],
        out_shape=[jax.ShapeDtypeStruct((T, D), jnp.float32), jax.ShapeDtypeStruct((T, D), jnp.bfloat16),
                   jax.ShapeDtypeStruct((E, T), jnp.float32), jax.ShapeDtypeStruct((2 * SUBLANES, T), jnp.bfloat16)],
        compiler_params=pltpu.CompilerParams(dimension_semantics=("arbitrary",), vmem_limit_bytes=VMEM_LIMIT),
        name="mixer_post",
    )(y, r, k, v, g, ga, mixb, x, *consts)


def _route_rows(h, rw2_ref, rwh_ref, rb_ref):
    E = rb_ref.shape[0]
    h_hi = h.astype(jnp.bfloat16)
    h_lo = (h - h_hi.astype(jnp.float32)).astype(jnp.bfloat16)
    lg = (jnp.dot(h_hi, rw2_ref[...], preferred_element_type=jnp.float32)
          + jnp.dot(h_lo, rwh_ref[...], preferred_element_type=jnp.float32))
    lg = jnp.transpose(lg)
    logits = lg[:E] + lg[E:2 * E]
    scores = _sigmoid(logits)
    sel = scores + rb_ref[...]
    per = E // N_EXPERT_GROUPS
    rows = [sel[e:e + 1, :] for e in range(E)]
    gscore = []
    for g in range(N_EXPERT_GROUPS):
        m = rows[g * per:(g + 1) * per]
        best = None
        for i in range(per):
            for j in range(i + 1, per):
                s2 = m[i] + m[j]
                best = s2 if best is None else jnp.maximum(best, s2)
        gscore.append(best)
    cur = gscore[0]
    bestg = jnp.zeros(cur.shape, jnp.int32)
    for g in range(1, N_EXPERT_GROUPS):
        take = gscore[g] > cur
        cur = jnp.where(take, gscore[g], cur)
        bestg = jnp.where(take, g, bestg)
    picked = []
    for e in range(E):
        g = e // per
        rank = jnp.zeros(cur.shape, jnp.int32)
        for j in range(g * per, (g + 1) * per):
            if j == e:
                continue
            ahead = (rows[j] > rows[e]) | ((rows[j] == rows[e]) & (j < e))
            rank = rank + ahead.astype(jnp.int32)
        picked.append(jnp.where((bestg == g) & (rank < 2), scores[e:e + 1, :], 0.0))
    total = picked[0]
    for e in range(1, E):
        total = total + picked[e]
    gsel = jnp.concatenate([jnp.where(bestg == g, 1.0, 0.0) for g in range(N_EXPERT_GROUPS)]
                           + [jnp.zeros_like(cur)] * (2 * SUBLANES - N_EXPERT_GROUPS), axis=0)
    return jnp.concatenate(picked, axis=0) / total, gsel


def _sort_kernel(gsel_ref, su_ref, srow_ref, scol_ref, cnt_ref):
    gsel = gsel_ref[...]
    pos = jnp.dot(gsel, su_ref[...], preferred_element_type=jnp.float32)[:SUBLANES]
    gsel = gsel[:SUBLANES].astype(jnp.float32)
    cnt = jnp.sum(gsel, axis=-1, keepdims=True)
    cnt_ref[0] = jnp.broadcast_to(cnt, (SUBLANES, LANES))
    padded = jnp.ceil(cnt * (1.0 / MOE_BLOCK)) * MOE_BLOCK
    rowi = lax.broadcasted_iota(jnp.int32, (SUBLANES, 1), 0)
    offs = jnp.zeros_like(cnt)
    run = jnp.zeros((1, 1), jnp.float32)
    for g in range(1, N_EXPERT_GROUPS):
        run = run + padded[g - 1:g, :]
        offs = jnp.where(rowi == g, run, offs)
    slot = jnp.sum(gsel * (pos + offs), axis=0, keepdims=True)
    srow_ref[...] = slot
    scol_ref[...] = jnp.transpose(jnp.broadcast_to(slot, (LANES, slot.shape[1])))


def _sort(gsel, su, tm):
    T = gsel.shape[1]
    f32 = jnp.float32
    return pl.pallas_call(
        _sort_kernel,
        grid=(T // tm,),
        in_specs=[pl.BlockSpec((gsel.shape[0], tm), lambda i: (0, i)), _const_spec(su.shape)],
        out_specs=[pl.BlockSpec((1, tm), lambda i: (0, i)), pl.BlockSpec((tm, LANES), lambda i: (i, 0)),
                   pl.BlockSpec((1, SUBLANES, LANES), lambda i: (i, 0, 0))],
        out_shape=[jax.ShapeDtypeStruct((1, T), f32), jax.ShapeDtypeStruct((T, LANES), f32),
                   jax.ShapeDtypeStruct((T // tm, SUBLANES, LANES), f32)],
        compiler_params=pltpu.CompilerParams(dimension_semantics=("arbitrary",), vmem_limit_bytes=VMEM_LIMIT),
        name="moe_sort",
    )(gsel, su)


def _moe_kernel(nblk_ref, off_ref, h_ref, wt_ref, srow_ref, scol_ref, eg_ref, eu_ref, ed_ref, o_ref, acc_ref,
                stage_ref):
    g = pl.program_id(0)
    i = pl.program_id(1)
    G = pl.num_programs(0)
    tm = h_ref.shape[0]
    per, _, F = eg_ref.shape
    bf, f32 = jnp.bfloat16, jnp.float32
    dot = functools.partial(jnp.dot, preferred_element_type=f32)
    acc_ref[...] = jnp.zeros_like(acc_ref)
    wt = wt_ref[...]
    w_hi = wt.astype(bf)
    w_r = wt - w_hi.astype(f32)
    w_mid = w_r.astype(bf)
    w_lo = (w_r - w_mid.astype(f32)).astype(bf)
    w_parts = jnp.concatenate([w_hi, w_mid, w_lo], axis=1)
    E = wt.shape[1]
    first = off_ref[i * G + g].astype(f32)

    def experts(base):
        srow = lax.broadcasted_iota(jnp.int32, (MOE_BLOCK, tm), 0).astype(f32) + base
        sel = jnp.where(srow == srow_ref[...], 1.0, 0.0).astype(bf)
        xb = dot(sel, h_ref[...]).astype(bf)
        wsp = dot(sel, w_parts)
        wsb = wsp[:, :E] + wsp[:, E:2 * E] + wsp[:, 2 * E:]
        lane = lax.broadcasted_iota(jnp.int32, wsb.shape, 1)
        out = None
        for e in range(per):
            wcol = jnp.sum(jnp.where(lane == g * per + e, wsb, 0.0), axis=-1, keepdims=True)
            gt = dot(xb, eg_ref[e])
            hid = gt * _sigmoid(gt) * dot(xb, eu_ref[e]) * wcol
            part = dot(hid.astype(bf), ed_ref[e])
            out = part if out is None else out + part
        return out.astype(bf)

    n = nblk_ref[i * G + g]

    def pair(jp, carry):
        base = first + (jp * (2 * MOE_BLOCK)).astype(f32)
        stage_ref[:MOE_BLOCK, :] = experts(base)
        second = 2 * jp + 1 < n

        @pl.when(second)
        def _():
            stage_ref[MOE_BLOCK:, :] = experts(base + float(MOE_BLOCK))

        @pl.when(jnp.logical_not(second))
        def _():
            stage_ref[MOE_BLOCK:, :] = jnp.zeros((MOE_BLOCK, stage_ref.shape[1]), bf)

        scol = lax.broadcasted_iota(jnp.int32, (tm, 2 * LANES), 1).astype(f32) + base
        mine = jnp.concatenate([scol_ref[...]] * 2, axis=1)
        back = jnp.where(scol == mine, 1.0, 0.0).astype(bf)
        acc_ref[...] += dot(back, stage_ref[...])
        return carry

    lax.fori_loop(0, (n + 1) // 2, pair, 0)
    o_ref[0] = acc_ref[...].astype(bf)


def _combine_kernel(alpha, x_ref, y_ref, gate_ref, lng_ref, lnb_ref, o_ref):
    o_ref[...] = _combine_rows(alpha, x_ref[...], y_ref, gate_ref, lng_ref, lnb_ref)


def _moe(h, wt, srow, scol, cnt, eg, eu, ed, tm):
    T, D = h.shape
    E, _, F = eg.shape
    G = N_EXPERT_GROUPS
    per = E // G
    nt = T // tm
    cnt_i = cnt[:, :G, 0].astype(jnp.int32)
    nblk = (cnt_i + MOE_BLOCK - 1) // MOE_BLOCK
    off = (jnp.cumsum(nblk, axis=1) - nblk) * MOE_BLOCK
    tok = lambda w: pl.BlockSpec((tm, w), lambda g, i, *_: (i, 0))
    grp = lambda a, b: pl.BlockSpec((per, a, b), lambda g, i, *_: (g, 0, 0), pipeline_mode=pl.Buffered(1))
    return pl.pallas_call(
        _moe_kernel,
        grid_spec=pltpu.PrefetchScalarGridSpec(
            num_scalar_prefetch=2,
            grid=(G, nt),
            in_specs=[tok(D), tok(E), pl.BlockSpec((1, tm), lambda g, i, *_: (0, i)), tok(LANES),
                      grp(D, F), grp(D, F), grp(F, D)],
            out_specs=pl.BlockSpec((1, tm, D), lambda g, i, *_: (g, i, 0)),
            scratch_shapes=[pltpu.VMEM((tm, D), jnp.float32), pltpu.VMEM((2 * MOE_BLOCK, D), jnp.bfloat16)]),
        out_shape=jax.ShapeDtypeStruct((G, T, D), jnp.bfloat16),
        compiler_params=pltpu.CompilerParams(dimension_semantics=("arbitrary", "arbitrary"),
                                             vmem_limit_bytes=VMEM_LIMIT),
        name="moe_experts",
    )(nblk.reshape(-1), off.reshape(-1), h, wt, srow, scol, eg, eu, ed)


def _combine(alpha, x, yg, gate, ln_g, ln_b, tm):
    T, D = x.shape
    vec = pl.BlockSpec((1, D), lambda i: (0, 0))
    return pl.pallas_call(
        functools.partial(_combine_kernel, alpha),
        grid=(T // tm,),
        in_specs=[pl.BlockSpec((tm, D), lambda i: (i, 0)), pl.BlockSpec((yg.shape[0], tm, D), lambda i: (0, i, 0)),
                  vec, vec, vec],
        out_specs=pl.BlockSpec((tm, D), lambda i: (i, 0)),
        out_shape=jax.ShapeDtypeStruct((T, D), jnp.float32),
        compiler_params=pltpu.CompilerParams(dimension_semantics=("arbitrary",), vmem_limit_bytes=VMEM_LIMIT),
        name="moe_combine",
    )(x, yg, gate, ln_g, ln_b)


def kernel(x, c, w_mod, b_mod, w_in, rwkv_mu, w0, w2, a0, a2, g2, k_k, k_a, r_k, gn_g, gn_b, vm0, vm1, vm2,
           sg_ln_g, sg_ln_b, w_s, b_s, w_branch_a, w_branch_b, w_out, ln1_g, ln1_b, router_w, router_b,
           e_gate, e_up, e_down, ln2_g, ln2_b):
    B, T, D = x.shape
    assert B == 1
    L = w_mod.shape[0]
    W = w0.shape[1]
    S = sg_ln_g.shape[1]
    dl, al = w2.shape[1], a2.shape[1]
    assert dl + al == LANES and W % LANES == 0 and S % LANES == 0
    rw_cols = rwkv_mu.shape[1]
    G, CH = w_s.shape[1], w_s.shape[2]
    assert CH == LANES and S // G == HEAD_DIM and W // r_k.shape[1] == HEAD_DIM
    alpha = float((2 * L) ** 0.25)
    bf = jnp.bfloat16
    f32 = jnp.float32

    mod = _modulation(c, w_mod, b_mod)
    head = jnp.arange(W) // HEAD_DIM
    hsum = (head[:, None] == head[None, :]).astype(bf)
    causal = jnp.tril(jnp.ones((CH, CH), bool))
    n_exp = router_w.shape[1]
    assert 2 * n_exp <= LANES
    rw_hi = router_w.astype(bf)
    rw_lo = (router_w - rw_hi.astype(f32)).astype(bf)
    rw2 = jnp.concatenate([rw_hi, rw_lo, jnp.zeros((D, LANES - 2 * n_exp), bf)], axis=1)
    rwh = jnp.concatenate([rw_hi, jnp.zeros((D, LANES - n_exp), bf)], axis=1)
    router_bc = router_b.reshape(-1, 1)
    w_in_bf = w_in.astype(bf)
    tidx = jnp.arange(MOE_TILE)
    su = (tidx[:, None] < tidx[None, :]).astype(bf)

    xt = x.reshape(T, D)
    v_first = None
    pending = None
    for i in range(L):
        sh1, sc1, gt1, sh2, sc2, gt2 = [mod[i, :, j * D:(j + 1) * D] for j in range(6)]
        w_m = jnp.where(causal, w_s[i], 0.0)
        p = dict(
            w_in=(w_in_bf, i),
            mu=rwkv_mu[i][None], w0=w0[i][None], a0=a0[i][None],
            w2p=jnp.concatenate([w2[i], jnp.zeros((al, W), f32)], 0).astype(bf),
            a2p=jnp.concatenate([jnp.zeros((dl, W), f32), a2[i]], 0).astype(bf),
            g2=g2[i].astype(bf), k_k=k_k[i][None], k_a=k_a[i][None], hsum=hsum,
            sg_ln_g=sg_ln_g[i][None], sg_ln_b=sg_ln_b[i][None],
            wpair=jnp.concatenate([w_m[0::2], w_m[1::2]], axis=2).astype(bf),
            sbias=jnp.repeat(b_s[i].T, HEAD_DIM, axis=1),
            w_bb=w_branch_b[i].astype(bf), w_ba=w_branch_a[i].astype(bf), w_out=w_out[i].astype(bf),
            r_k=r_k[i].reshape(1, W), gn_g=gn_g[i][None], gn_b=gn_b[i][None],
            ln1_g=ln1_g[i][None], ln1_b=ln1_b[i][None],
        )
        if i > 0:
            p.update(vm0=vm0[i - 1][None], vm1=vm1[i - 1].astype(bf), vm2=vm2[i - 1].astype(bf))
        outs = _front(xt, sh1, sc1, p, v_first, tm=512, prev=pending)
        r, k, v, kk, b, lw, g, ga, mixb = outs[:9]
        if pending is not None:
            xt = outs[9]
        if i == 0:
            v_first = v
        y = _scan(r, k, v, kk, b, lw, tt=256)
        xt, h, wt, gsel = _post(alpha, y, r, k, v, g, ga, mixb, xt, gt1, p, sh2, sc2, rw2, rwh, router_bc, tm=512)
        srow, scol, cnt = _sort(gsel, su, tm=MOE_TILE)
        yg = _moe(h, wt.T, srow, scol, cnt, e_gate[i].astype(bf), e_up[i].astype(bf), e_down[i].astype(bf),
                  tm=MOE_TILE)
        pending = (alpha, yg, gt2, ln2_g[i][None], ln2_b[i][None])
    xt = _combine(*pending[:1], xt, *pending[1:], tm=512)
    return xt.reshape(B, T, D)
```

```python
import functools

import jax
import jax.numpy as jnp
from jax import lax
from jax.experimental import pallas as pl
from jax.experimental.pallas import tpu as pltpu

LN_EPS = 1e-5
GN_EPS = 64e-5
N_EXPERT_GROUPS = 4
LANES = 128
SUBLANES = 8
MOE_BLOCK = 128
SCAN_CHUNK = 64
HEAD_DIM = 64
MOE_TILE = 1024
VMEM_LIMIT = 56 * 1024 * 1024
FRONT_VMEM_LIMIT = 60 * 1024 * 1024

_NT = (((1,), (1,)), ((), ()))
_TN = (((0,), (0,)), ((), ()))


def _bdot(a, b):
    return jnp.dot(a.astype(jnp.bfloat16), b.astype(jnp.bfloat16), preferred_element_type=jnp.float32)


def _split_dot(a, b_exact, terms=2):
    acc = None
    rem = a
    for _ in range(terms):
        piece = rem.astype(jnp.bfloat16)
        part = jnp.dot(piece, b_exact, preferred_element_type=jnp.float32)
        acc = part if acc is None else acc + part
        rem = rem - piece.astype(jnp.float32)
    return acc


def _sigmoid(x):
    return 1.0 / (1.0 + jnp.exp(-x))


def _softplus(x):
    return jnp.maximum(x, 0.0) + jnp.log(1.0 + jnp.exp(-jnp.abs(x)))


def _ln_rows(x, eps):
    mu = jnp.mean(x, axis=-1, keepdims=True)
    xc = x - mu
    var = jnp.mean(xc * xc, axis=-1, keepdims=True)
    return xc * lax.rsqrt(var + eps)


def _combine_rows(alpha, x, yg_ref, gate_ref, lng_ref, lnb_ref):
    y = yg_ref[0].astype(jnp.float32)
    for g in range(1, yg_ref.shape[0]):
        y = y + yg_ref[g].astype(jnp.float32)
    xn = alpha * x + (1.0 + gate_ref[...]) * y
    return _ln_rows(xn, LN_EPS) * lng_ref[...] + lnb_ref[...]


def _const_spec(shape):
    nd = len(shape)
    return pl.BlockSpec(shape, lambda *_: (0,) * nd, pipeline_mode=pl.Buffered(1))


def _mod_kernel(c_ref, w_ref, b_ref, o_ref):
    c = c_ref[...]
    cond = c * _sigmoid(c)
    o_ref[0] = jnp.sum(cond * w_ref[0], axis=0, keepdims=True) + b_ref[0]


def _modulation(c, w_mod, b_mod):
    L, D, D6 = w_mod.shape
    nb = D6 // D
    out = pl.pallas_call(
        _mod_kernel,
        grid=(L, nb),
        in_specs=[pl.BlockSpec((D, 1), lambda l, j: (0, 0)),
                  pl.BlockSpec((1, D, D), lambda l, j: (l, 0, j)),
                  pl.BlockSpec((1, 1, D), lambda l, j: (l, 0, j))],
        out_specs=pl.BlockSpec((1, 1, D), lambda l, j: (l, 0, j)),
        out_shape=jax.ShapeDtypeStruct((L, 1, D6), jnp.float32),
        compiler_params=pltpu.CompilerParams(dimension_semantics=("arbitrary", "arbitrary"),
                                             vmem_limit_bytes=VMEM_LIMIT),
        name="adaln_mod",
    )(c.reshape(D, 1), w_mod, b_mod.reshape(L, 1, D6))
    return out


def _front_kernel(has_vmix, prev_alpha, W, *refs):
    x_ref = refs[0]
    pos = 1
    if prev_alpha is not None:
        yg_ref, pgate_ref, plng_ref, plnb_ref = refs[pos:pos + 4]
        pos += 4
    (sh_ref, sc_ref, win_ref, mu_ref, w0_ref, w2p_ref, a0_ref, a2p_ref, g2_ref,
     kkw_ref, ka_ref, hsum_ref) = refs[pos:pos + 12]
    pos += 12
    if has_vmix:
        vf_ref, vm0_ref, vm1_ref, vm2_ref = refs[pos:pos + 4]
        pos += 4
    lng_ref, lnb_ref, wpair_ref, sbias_ref, wbb_ref = refs[pos:pos + 5]
    pos += 5
    r_o, k_o, v_o, kk_o, b_o, lw_o, g_o, ga_o, mixb_o = refs[pos:pos + 9]
    pos += 9
    rw_cols = mu_ref.shape[1]
    sg_cols = 2 * lng_ref.shape[1]
    wrw_ref = win_ref.at[0, :, :rw_cols]
    wsg_ref = win_ref.at[0, :, rw_cols:rw_cols + sg_cols]
    wgt_ref = win_ref.at[0, :, rw_cols + sg_cols:]
    if prev_alpha is not None:
        x_o = refs[pos]
        pos += 1
    carry_ref = refs[pos]

    tm = x_ref.shape[0]

    @pl.when(pl.program_id(0) == 0)
    def _():
        carry_ref[...] = jnp.zeros_like(carry_ref)

    x = x_ref[...]
    if prev_alpha is not None:
        x = _combine_rows(prev_alpha, x, yg_ref, pgate_ref, plng_ref, plnb_ref)
        x_o[...] = x
    h = _ln_rows(x, LN_EPS) * (1.0 + sc_ref[...]) + sh_ref[...]
    hb = h.astype(jnp.bfloat16)

    z = jnp.dot(hb, wrw_ref[...], preferred_element_type=jnp.float32)
    row = lax.broadcasted_iota(jnp.int32, z.shape, 0)
    prev = jnp.where(row == 0, carry_ref[...], pltpu.roll(z, 1, axis=0))
    carry_ref[...] = z[tm - 1:tm, :]
    z = z + (prev - z) * mu_ref[...]
    r = z[:, :W]
    k = z[:, W:2 * W]
    v = z[:, 2 * W:3 * W]
    zwa = z[:, 3 * W:3 * W + LANES]
    zg = z[:, 3 * W + LANES:]
    w_log = -_softplus(-(w0_ref[...] + _bdot(jnp.tanh(zwa), w2p_ref[...]))) - 0.5
    lw_o[...] = -jnp.exp(w_log)
    a = _sigmoid(a0_ref[...] + _bdot(zwa, a2p_ref[...]))
    g_o[...] = _bdot(_sigmoid(zg), g2_ref[...])
    if has_vmix:
        mix = _sigmoid(vm0_ref[...] + _bdot(_bdot(v, vm1_ref[...]), vm2_ref[...]))
        v = v + (vf_ref[...] - v) * mix
    kk = k * kkw_ref[...]
    ss = _split_dot(kk * kk, hsum_ref[...], terms=1)
    kk = kk / jnp.maximum(jnp.sqrt(ss), 1e-12)
    r_o[...] = r
    k_o[...] = k * (1.0 + (a - 1.0) * ka_ref[...])
    v_o[...] = v
    kk_o[...] = kk
    b_o[...] = kk * a

    zs = jnp.dot(hb, wsg_ref[...], preferred_element_type=jnp.float32)
    zs = zs * (0.5 * (1.0 + jnp.tanh(0.7978845608028654 * (zs + 0.044715 * (zs * zs * zs)))))
    S = zs.shape[1] // 2
    u = zs[:, :S]
    vln = _ln_rows(zs[:, S:], LN_EPS) * lng_ref[...] + lnb_ref[...]
    lo = lax.broadcasted_iota(jnp.int32, (LANES, LANES), 1) < HEAD_DIM
    chunks = []
    for c in range(tm // LANES):
        cols = []
        for p in range(S // LANES):
            vp = vln[c * LANES:(c + 1) * LANES, p * LANES:(p + 1) * LANES]
            vs = jnp.concatenate([jnp.where(lo, vp, 0.0), jnp.where(lo, 0.0, vp)], axis=0)
            cols.append(_bdot(wpair_ref[p], vs))
        chunks.append(jnp.concatenate(cols, axis=1) + sbias_ref[...])
    yb = u * jnp.concatenate(chunks, axis=0)

    zg2 = jnp.dot(hb, wgt_ref[...], preferred_element_type=jnp.float32)
    Dm = zg2.shape[1] // 2
    ga_o[...] = _sigmoid(zg2[:, :Dm])
    mixb_o[...] = _sigmoid(zg2[:, Dm:]) * _bdot(yb, wbb_ref[...])


def _front(x, shift, scale, p, v_first, tm, prev=None):
    T, D = x.shape
    W = p["hsum"].shape[0]
    has_vmix = v_first is not None
    row = lambda w: pl.BlockSpec((tm, w), lambda i: (i, 0))
    ins = [x]
    specs = [row(D)]
    if prev is not None:
        ins += list(prev[1:])
        specs += [pl.BlockSpec((prev[1].shape[0], tm, D), lambda i: (0, i, 0))] + [_const_spec(a.shape)
                                                                                   for a in prev[2:]]
    w_in, layer = p["w_in"]
    consts = [shift, scale, w_in, p["mu"], p["w0"], p["w2p"], p["a0"], p["a2p"], p["g2"],
              p["k_k"], p["k_a"], p["hsum"]]
    specs += [_const_spec(a.shape) for a in consts]
    specs[len(ins) + 2] = pl.BlockSpec((1,) + w_in.shape[1:], lambda i: (layer, 0, 0),
                                       pipeline_mode=pl.Buffered(1))
    ins += consts
    if has_vmix:
        extra = [v_first, p["vm0"], p["vm1"], p["vm2"]]
        ins += extra
        specs += [row(W)] + [_const_spec(a.shape) for a in extra[1:]]
    tail = [p["sg_ln_g"], p["sg_ln_b"], p["wpair"], p["sbias"], p["w_bb"]]
    ins += tail
    specs += [_const_spec(a.shape) for a in tail]
    f32 = jnp.float32
    n_wide = 2 if prev is None else 3
    out_shape = [jax.ShapeDtypeStruct((T, W), f32)] * 7 + [jax.ShapeDtypeStruct((T, D), f32)] * n_wide
    out_specs = [row(W)] * 7 + [row(D)] * n_wide
    return pl.pallas_call(
        functools.partial(_front_kernel, has_vmix, None if prev is None else prev[0], W),
        grid=(T // tm,),
        in_specs=specs,
        out_specs=out_specs,
        out_shape=out_shape,
        scratch_shapes=[pltpu.VMEM((1, p["mu"].shape[1]), f32)],
        compiler_params=pltpu.CompilerParams(dimension_semantics=("arbitrary",), vmem_limit_bytes=FRONT_VMEM_LIMIT),
        name="mixer_front",
    )(*ins)


def _scan_kernel(r_ref, k_ref, v_ref, kk_ref, b_ref, lw_ref, y_ref, s_ref):
    C = SCAN_CHUNK
    P = LANES
    bf = jnp.bfloat16
    tt, width = lw_ref.shape
    n_chunks = tt // C
    n_blocks = width // P

    @pl.when(pl.program_id(0) == 0)
    def _():
        s_ref[...] = jnp.zeros_like(s_ref)

    lw = lw_ref[...]
    ti = lax.broadcasted_iota(jnp.int32, (tt, tt), 0)
    tj = lax.broadcasted_iota(jnp.int32, (tt, tt), 1)
    same_chunk = (ti // C) == (tj // C)
    pre = jnp.where(same_chunk & (ti >= tj), 1.0, 0.0).astype(bf)
    cum = None
    rem = lw
    for _ in range(3):
        piece = rem.astype(bf)
        part = jnp.dot(pre, piece, preferred_element_type=jnp.float32)
        cum = part if cum is None else cum + part
        rem = rem - piece.astype(jnp.float32)
    tot = jnp.concatenate([jnp.broadcast_to(cum[(j + 1) * C - 1:(j + 1) * C, :], (C, width))
                           for j in range(n_chunks)], axis=0)
    w_inv = jnp.exp(-cum)
    w_end = jnp.exp(tot - cum)
    w_tot = jnp.exp(tot)

    kk = kk_ref[...]
    bb = b_ref[...]
    kx = k_ref[...]
    a_t = -kk * jnp.exp(cum - lw)
    b_t = (bb * w_inv).astype(bf)
    k_t = (kx * w_inv).astype(bf)
    r_t = r_ref[...] * jnp.exp(cum)
    b_h = (bb * w_end).astype(bf)
    k_h = (kx * w_end).astype(bf)
    vv = v_ref[...]

    ri = lax.broadcasted_iota(jnp.int32, (P, P), 0)
    ci = lax.broadcasted_iota(jnp.int32, (P, P), 1)
    same = (ri >= C) == (ci >= C)
    strict = same & (ri > ci)
    incl = same & (ri >= ci)
    eye = ri == ci
    lo = lax.broadcasted_iota(jnp.int32, (C, P), 1) < HEAD_DIM

    def stack(m):
        return jnp.concatenate([jnp.where(lo, m, 0.0), jnp.where(lo, 0.0, m)], axis=0)

    def fold(m):
        return m[:C] + m[C:]

    def mm(a, b, dims=(((1,), (0,)), ((), ()))):
        return lax.dot_general(a, b, dims, preferred_element_type=jnp.float32)

    chains = [(j, p) for j in range(n_chunks) for p in range(n_blocks)]
    n = len(chains)
    cut = lambda m, j, p: m[j * C:(j + 1) * C, p * P:(p + 1) * P]
    a_s = [stack(cut(a_t, j, p)).astype(bf) for j, p in chains]
    r_s = [stack(cut(r_t, j, p)) for j, p in chains]
    v_s = [stack(cut(vv, j, p)).astype(bf) for j, p in chains]
    q = [mm(jnp.concatenate([a_s[i], r_s[i].astype(bf)], axis=0),
            jnp.concatenate([cut(b_t, j, p)] * 2 + [cut(k_t, j, p)] * 2, axis=0), _NT)
         for i, (j, p) in enumerate(chains)]
    l_ab = [jnp.where(strict, m[:P, :P], 0.0) for m in q]
    low = [jnp.concatenate([jnp.where(strict, m[:P, P:], 0.0), jnp.where(incl, m[P:, P:], 0.0)], axis=0).astype(bf)
           for m in q]
    p_rb = [jnp.where(incl, m[P:, :P], 0.0).astype(bf) for m in q]
    inv = [jnp.where(eye, 1.0, jnp.where((ri >> 1) == (ci >> 1), m, 0.0)) for m in l_ab]
    for lg in range(1, 6):
        off = ((ri >> (lg + 1)) == (ci >> (lg + 1))) & (((ri >> lg) & 1) == 1) & (((ci >> lg) & 1) == 0)
        inv_b = [m.astype(bf) for m in inv]
        tmp = [mm(jnp.where(off, l_ab[i], 0.0).astype(bf), inv_b[i]) for i in range(n)]
        inv = [inv[i] + mm(inv_b[i], tmp[i].astype(bf)) for i in range(n)]
    lv = [mm(low[i], v_s[i]) for i in range(n)]
    ta = [mm(inv[i].astype(bf), jnp.concatenate([a_s[i], lv[i][:P].astype(bf)], axis=1))
          for i in range(n)]
    pa = [mm(p_rb[i], ta[i].astype(bf)) + jnp.concatenate([r_s[i], lv[i][P:]], axis=1) for i in range(n)]
    a_p = [fold(m[:, :P]) for m in ta]
    u_v = [fold(m[:, P:]) for m in ta]
    r_p = [fold(m[:, :P]).astype(bf) for m in pa]
    y_loc = [fold(m[:, P:]) for m in pa]
    zero = jnp.zeros((C, P), jnp.float32)
    gh = [mm(jnp.concatenate([cut(b_h, j, p), cut(k_h, j, p)], axis=0),
             jnp.concatenate([jnp.concatenate([a_p[i], u_v[i]], axis=1),
                              jnp.concatenate([zero, cut(vv, j, p)], axis=1)], axis=0).astype(bf), _TN)
          for i, (j, p) in enumerate(chains)]
    g_m = [(jnp.where(same, gh[i][:, :P], 0.0) + jnp.where(eye, cut(w_tot, j, p)[:1], 0.0)).astype(bf)
           for i, (j, p) in enumerate(chains)]
    h_m = [jnp.where(same, m[:, P:], 0.0) for m in gh]
    state = [s_ref[p] for p in range(n_blocks)]
    for i, (j, p) in enumerate(chains):
        s_b = state[p].astype(bf)
        y_ref[j * C:(j + 1) * C, p * P:(p + 1) * P] = mm(r_p[i], s_b) + y_loc[i]
        state[p] = mm(g_m[i], s_b) + h_m[i]
    for p in range(n_blocks):
        s_ref[p] = state[p]


def _scan(r, k, v, kk, b, lw, tt):
    T, W = r.shape
    spec = pl.BlockSpec((tt, W), lambda i: (i, 0))
    return pl.pallas_call(
        _scan_kernel,
        grid=(T // tt,),
        in_specs=[spec] * 6,
        out_specs=spec,
        out_shape=jax.ShapeDtypeStruct((T, W), jnp.float32),
        scratch_shapes=[pltpu.VMEM((W // LANES, LANES, LANES), jnp.float32)],
        compiler_params=pltpu.CompilerParams(dimension_semantics=("arbitrary",), vmem_limit_bytes=VMEM_LIMIT),
        name="wkv7_scan",
    )(r, k, v, kk, b, lw)


def _post_kernel(alpha, y_ref, r_ref, k_ref, v_ref, g_ref, ga_ref, mixb_ref, x_ref, rk_ref, gng_ref, gnb_ref,
                 hsum_ref, wba_ref, wout_ref, gate_ref, lng_ref, lnb_ref, sh2_ref, sc2_ref, rw2_ref, rwh_ref, rb_ref,
                 o_ref, h_ref, wt_ref, gsel_ref):
    y = y_ref[...]
    hs = hsum_ref[...]
    inv_n = 1.0 / HEAD_DIM
    mu = _split_dot(y, hs) * inv_n
    yc = y - mu
    var = _split_dot(yc * yc, hs, terms=1) * inv_n
    yn = yc * lax.rsqrt(var + GN_EPS) * gng_ref[...] + gnb_ref[...]
    v = v_ref[...]
    bonus = _split_dot(r_ref[...] * k_ref[...] * rk_ref[...], hs, terms=1)
    ya = (yn + bonus * v) * g_ref[...]
    mixed = ga_ref[...] * _bdot(ya, wba_ref[...]) + mixb_ref[...]
    mixed = _bdot(mixed, wout_ref[...])
    xn = alpha * x_ref[...] + (1.0 + gate_ref[...]) * mixed
    x_new = _ln_rows(xn, LN_EPS) * lng_ref[...] + lnb_ref[...]
    o_ref[...] = x_new
    h = _ln_rows(x_new, LN_EPS) * (1.0 + sc2_ref[...]) + sh2_ref[...]
    h_ref[...] = h.astype(jnp.bfloat16)
    wt, gsel = _route_rows(h, rw2_ref, rwh_ref, rb_ref)
    wt_ref[...] = wt
    gsel_ref[...] = gsel.astype(jnp.bfloat16)


def _post(alpha, y, r, k, v, g, ga, mixb, x, gate, p, shift2, scale2, rw2, rwh, router_b, tm):
    T, D = x.shape
    W = y.shape[1]
    E = router_b.shape[0]
    row = lambda w: pl.BlockSpec((tm, w), lambda i: (i, 0))
    col = lambda n: pl.BlockSpec((n, tm), lambda i: (0, i))
    consts = [p["r_k"], p["gn_g"], p["gn_b"], p["hsum"], p["w_ba"], p["w_out"], gate, p["ln1_g"], p["ln1_b"],
              shift2, scale2, rw2, rwh, router_b]
    return pl.pallas_call(
        functools.partial(_post_kernel, alpha),
        grid=(T // tm,),
        in_specs=[row(W)] * 5 + [row(D)] * 3 + [_const_spec(a.shape) for a in consts],
        out_specs=[row(D), row(D), col(E), col(2 * SUBLANES)],
        out_shape=[jax.ShapeDtypeStruct((T, D), jnp.float32), jax.ShapeDtypeStruct((T, D), jnp.bfloat16),
                   jax.ShapeDtypeStruct((E, T), jnp.float32), jax.ShapeDtypeStruct((2 * SUBLANES, T), jnp.bfloat16)],
        compiler_params=pltpu.CompilerParams(dimension_semantics=("arbitrary",), vmem_limit_bytes=VMEM_LIMIT),
        name="mixer_post",
    )(y, r, k, v, g, ga, mixb, x, *consts)


def _route_rows(h, rw2_ref, rwh_ref, rb_ref):
    E = rb_ref.shape[0]
    h_hi = h.astype(jnp.bfloat16)
    h_lo = (h - h_hi.astype(jnp.float32)).astype(jnp.bfloat16)
    lg = (jnp.dot(h_hi, rw2_ref[...], preferred_element_type=jnp.float32)
          + jnp.dot(h_lo, rwh_ref[...], preferred_element_type=jnp.float32))
    lg = jnp.transpose(lg)
    logits = lg[:E] + lg[E:2 * E]
    scores = _sigmoid(logits)
    sel = scores + rb_ref[...]
    per = E // N_EXPERT_GROUPS
    rows = [sel[e:e + 1, :] for e in range(E)]
    gscore = []
    for g in range(N_EXPERT_GROUPS):
        m = rows[g * per:(g + 1) * per]
        best = None
        for i in range(per):
            for j in range(i + 1, per):
                s2 = m[i] + m[j]
                best = s2 if best is None else jnp.maximum(best, s2)
        gscore.append(best)
    cur = gscore[0]
    bestg = jnp.zeros(cur.shape, jnp.int32)
    for g in range(1, N_EXPERT_GROUPS):
        take = gscore[g] > cur
        cur = jnp.where(take, gscore[g], cur)
        bestg = jnp.where(take, g, bestg)
    picked = []
    for e in range(E):
        g = e // per
        rank = jnp.zeros(cur.shape, jnp.int32)
        for j in range(g * per, (g + 1) * per):
            if j == e:
                continue
            ahead = (rows[j] > rows[e]) | ((rows[j] == rows[e]) & (j < e))
            rank = rank + ahead.astype(jnp.int32)
        picked.append(jnp.where((bestg == g) & (rank < 2), scores[e:e + 1, :], 0.0))
    total = picked[0]
    for e in range(1, E):
        total = total + picked[e]
    gsel = jnp.concatenate([jnp.where(bestg == g, 1.0, 0.0) for g in range(N_EXPERT_GROUPS)]
                           + [jnp.zeros_like(cur)] * (2 * SUBLANES - N_EXPERT_GROUPS), axis=0)
    return jnp.concatenate(picked, axis=0) / total, gsel


def _sort_kernel(gsel_ref, su_ref, srow_ref, scol_ref, cnt_ref):
    gsel = gsel_ref[...]
    pos = jnp.dot(gsel, su_ref[...], preferred_element_type=jnp.float32)[:SUBLANES]
    gsel = gsel[:SUBLANES].astype(jnp.float32)
    cnt = jnp.sum(gsel, axis=-1, keepdims=True)
    cnt_ref[0] = jnp.broadcast_to(cnt, (SUBLANES, LANES))
    padded = jnp.ceil(cnt * (1.0 / MOE_BLOCK)) * MOE_BLOCK
    rowi = lax.broadcasted_iota(jnp.int32, (SUBLANES, 1), 0)
    offs = jnp.zeros_like(cnt)
    run = jnp.zeros((1, 1), jnp.float32)
    for g in range(1, N_EXPERT_GROUPS):
        run = run + padded[g - 1:g, :]
        offs = jnp.where(rowi == g, run, offs)
    slot = jnp.sum(gsel * (pos + offs), axis=0, keepdims=True)
    srow_ref[...] = slot
    scol_ref[...] = jnp.transpose(jnp.broadcast_to(slot, (LANES, slot.shape[1])))


def _sort(gsel, su, tm):
    T = gsel.shape[1]
    f32 = jnp.float32
    return pl.pallas_call(
        _sort_kernel,
        grid=(T // tm,),
        in_specs=[pl.BlockSpec((gsel.shape[0], tm), lambda i: (0, i)), _const_spec(su.shape)],
        out_specs=[pl.BlockSpec((1, tm), lambda i: (0, i)), pl.BlockSpec((tm, LANES), lambda i: (i, 0)),
                   pl.BlockSpec((1, SUBLANES, LANES), lambda i: (i, 0, 0))],
        out_shape=[jax.ShapeDtypeStruct((1, T), f32), jax.ShapeDtypeStruct((T, LANES), f32),
                   jax.ShapeDtypeStruct((T // tm, SUBLANES, LANES), f32)],
        compiler_params=pltpu.CompilerParams(dimension_semantics=("arbitrary",), vmem_limit_bytes=VMEM_LIMIT),
        name="moe_sort",
    )(gsel, su)


def _moe_kernel(nblk_ref, off_ref, h_ref, wt_ref, srow_ref, scol_ref, eg_ref, eu_ref, ed_ref, o_ref, acc_ref,
                stage_ref):
    g = pl.program_id(0)
    i = pl.program_id(1)
    G = pl.num_programs(0)
    tm = h_ref.shape[0]
    per, _, F = eg_ref.shape[1:]
    bf, f32 = jnp.bfloat16, jnp.float32
    dot = functools.partial(jnp.dot, preferred_element_type=f32)
    acc_ref[...] = jnp.zeros_like(acc_ref)
    wt = wt_ref[...]
    w_hi = wt.astype(bf)
    w_r = wt - w_hi.astype(f32)
    w_mid = w_r.astype(bf)
    w_lo = (w_r - w_mid.astype(f32)).astype(bf)
    w_parts = jnp.concatenate([w_hi, w_mid, w_lo], axis=1)
    E = wt.shape[1]
    first = off_ref[i * G + g].astype(f32)

    def experts(base):
        srow = lax.broadcasted_iota(jnp.int32, (MOE_BLOCK, tm), 0).astype(f32) + base
        sel = jnp.where(srow == srow_ref[...], 1.0, 0.0).astype(bf)
        xb = dot(sel, h_ref[...]).astype(bf)
        wsp = dot(sel, w_parts)
        wsb = wsp[:, :E] + wsp[:, E:2 * E] + wsp[:, 2 * E:]
        lane = lax.broadcasted_iota(jnp.int32, wsb.shape, 1)
        out = None
        for e in range(per):
            wcol = jnp.sum(jnp.where(lane == g * per + e, wsb, 0.0), axis=-1, keepdims=True)
            gt = dot(xb, eg_ref[0, e])
            hid = gt * _sigmoid(gt) * dot(xb, eu_ref[0, e]) * wcol
            part = dot(hid.astype(bf), ed_ref[0, e])
            out = part if out is None else out + part
        return out.astype(bf)

    n = nblk_ref[i * G + g]

    def pair(jp, carry):
        base = first + (jp * (2 * MOE_BLOCK)).astype(f32)
        stage_ref[:MOE_BLOCK, :] = experts(base)
        second = 2 * jp + 1 < n

        @pl.when(second)
        def _():
            stage_ref[MOE_BLOCK:, :] = experts(base + float(MOE_BLOCK))

        @pl.when(jnp.logical_not(second))
        def _():
            stage_ref[MOE_BLOCK:, :] = jnp.zeros((MOE_BLOCK, stage_ref.shape[1]), bf)

        scol = lax.broadcasted_iota(jnp.int32, (tm, 2 * LANES), 1).astype(f32) + base
        mine = jnp.concatenate([scol_ref[...]] * 2, axis=1)
        back = jnp.where(scol == mine, 1.0, 0.0).astype(bf)
        acc_ref[...] += dot(back, stage_ref[...])
        return carry

    lax.fori_loop(0, (n + 1) // 2, pair, 0)
    o_ref[0] = acc_ref[...].astype(bf)


def _combine_kernel(alpha, x_ref, y_ref, gate_ref, lng_ref, lnb_ref, o_ref):
    o_ref[...] = _combine_rows(alpha, x_ref[...], y_ref, gate_ref, lng_ref, lnb_ref)


def _moe(h, wt, srow, scol, cnt, eg, eu, ed, layer, tm):
    T, D = h.shape
    E, _, F = eg.shape[1:]
    G = N_EXPERT_GROUPS
    per = E // G
    nt = T // tm
    cnt_i = cnt[:, :G, 0].astype(jnp.int32)
    nblk = (cnt_i + MOE_BLOCK - 1) // MOE_BLOCK
    off = (jnp.cumsum(nblk, axis=1) - nblk) * MOE_BLOCK
    tok = lambda w: pl.BlockSpec((tm, w), lambda g, i, *_: (i, 0))
    grp = lambda a, b: pl.BlockSpec((1, per, a, b), lambda g, i, *_: (layer, g, 0, 0),
                                    pipeline_mode=pl.Buffered(1))
    return pl.pallas_call(
        _moe_kernel,
        grid_spec=pltpu.PrefetchScalarGridSpec(
            num_scalar_prefetch=2,
            grid=(G, nt),
            in_specs=[tok(D), tok(E), pl.BlockSpec((1, tm), lambda g, i, *_: (0, i)), tok(LANES),
                      grp(D, F), grp(D, F), grp(F, D)],
            out_specs=pl.BlockSpec((1, tm, D), lambda g, i, *_: (g, i, 0)),
            scratch_shapes=[pltpu.VMEM((tm, D), jnp.float32), pltpu.VMEM((2 * MOE_BLOCK, D), jnp.bfloat16)]),
        out_shape=jax.ShapeDtypeStruct((G, T, D), jnp.bfloat16),
        compiler_params=pltpu.CompilerParams(dimension_semantics=("arbitrary", "arbitrary"),
                                             vmem_limit_bytes=VMEM_LIMIT),
        name="moe_experts",
    )(nblk.reshape(-1), off.reshape(-1), h, wt, srow, scol, eg, eu, ed)


def _combine(alpha, x, yg, gate, ln_g, ln_b, tm):
    T, D = x.shape
    vec = pl.BlockSpec((1, D), lambda i: (0, 0))
    return pl.pallas_call(
        functools.partial(_combine_kernel, alpha),
        grid=(T // tm,),
        in_specs=[pl.BlockSpec((tm, D), lambda i: (i, 0)), pl.BlockSpec((yg.shape[0], tm, D), lambda i: (0, i, 0)),
                  vec, vec, vec],
        out_specs=pl.BlockSpec((tm, D), lambda i: (i, 0)),
        out_shape=jax.ShapeDtypeStruct((T, D), jnp.float32),
        compiler_params=pltpu.CompilerParams(dimension_semantics=("arbitrary",), vmem_limit_bytes=VMEM_LIMIT),
        name="moe_combine",
    )(x, yg, gate, ln_g, ln_b)


def kernel(x, c, w_mod, b_mod, w_in, rwkv_mu, w0, w2, a0, a2, g2, k_k, k_a, r_k, gn_g, gn_b, vm0, vm1, vm2,
           sg_ln_g, sg_ln_b, w_s, b_s, w_branch_a, w_branch_b, w_out, ln1_g, ln1_b, router_w, router_b,
           e_gate, e_up, e_down, ln2_g, ln2_b):
    B, T, D = x.shape
    assert B == 1
    L = w_mod.shape[0]
    W = w0.shape[1]
    S = sg_ln_g.shape[1]
    dl, al = w2.shape[1], a2.shape[1]
    assert dl + al == LANES and W % LANES == 0 and S % LANES == 0
    G, CH = w_s.shape[1], w_s.shape[2]
    assert CH == LANES and S // G == HEAD_DIM and W // r_k.shape[1] == HEAD_DIM
    assert rwkv_mu.shape[1] == 3 * W + LANES + g2.shape[1] and w_in.shape[2] == rwkv_mu.shape[1] + 2 * S + 2 * D
    assert T % MOE_TILE == 0
    alpha = float((2 * L) ** 0.25)
    bf = jnp.bfloat16
    f32 = jnp.float32

    mod = _modulation(c, w_mod, b_mod)
    head = jnp.arange(W) // HEAD_DIM
    hsum = (head[:, None] == head[None, :]).astype(bf)
    causal = jnp.tril(jnp.ones((CH, CH), bool))
    n_exp = router_w.shape[1]
    assert 2 * n_exp <= LANES
    rw_hi = router_w.astype(bf)
    rw_lo = (router_w - rw_hi.astype(f32)).astype(bf)
    rw2 = jnp.concatenate([rw_hi, rw_lo, jnp.zeros((D, LANES - 2 * n_exp), bf)], axis=1)
    rwh = jnp.concatenate([rw_hi, jnp.zeros((D, LANES - n_exp), bf)], axis=1)
    router_bc = router_b.reshape(-1, 1)
    w_in_bf = w_in.astype(bf)
    eg_bf, eu_bf, ed_bf = e_gate.astype(bf), e_up.astype(bf), e_down.astype(bf)
    tidx = jnp.arange(MOE_TILE)
    su = (tidx[:, None] < tidx[None, :]).astype(bf)

    xt = x.reshape(T, D)
    v_first = None
    pending = None
    for i in range(L):
        sh1, sc1, gt1, sh2, sc2, gt2 = [mod[i, :, j * D:(j + 1) * D] for j in range(6)]
        w_m = jnp.where(causal, w_s[i], 0.0)
        p = dict(
            w_in=(w_in_bf, i),
            mu=rwkv_mu[i][None], w0=w0[i][None], a0=a0[i][None],
            w2p=jnp.concatenate([w2[i], jnp.zeros((al, W), f32)], 0).astype(bf),
            a2p=jnp.concatenate([jnp.zeros((dl, W), f32), a2[i]], 0).astype(bf),
            g2=g2[i].astype(bf), k_k=k_k[i][None], k_a=k_a[i][None], hsum=hsum,
            sg_ln_g=sg_ln_g[i][None], sg_ln_b=sg_ln_b[i][None],
            wpair=jnp.concatenate([w_m[0::2], w_m[1::2]], axis=2).astype(bf),
            sbias=jnp.repeat(b_s[i].T, HEAD_DIM, axis=1),
            w_bb=w_branch_b[i].astype(bf), w_ba=w_branch_a[i].astype(bf), w_out=w_out[i].astype(bf),
            r_k=r_k[i].reshape(1, W), gn_g=gn_g[i][None], gn_b=gn_b[i][None],
            ln1_g=ln1_g[i][None], ln1_b=ln1_b[i][None],
        )
        if i > 0:
            p.update(vm0=vm0[i - 1][None], vm1=vm1[i - 1].astype(bf), vm2=vm2[i - 1].astype(bf))
        outs = _front(xt, sh1, sc1, p, v_first, tm=512, prev=pending)
        r, k, v, kk, b, lw, g, ga, mixb = outs[:9]
        if pending is not None:
            xt = outs[9]
        if i == 0:
            v_first = v
        y = _scan(r, k, v, kk, b, lw, tt=256)
        xt, h, wt, gsel = _post(alpha, y, r, k, v, g, ga, mixb, xt, gt1, p, sh2, sc2, rw2, rwh, router_bc, tm=512)
        srow, scol, cnt = _sort(gsel, su, tm=MOE_TILE)
        yg = _moe(h, wt.T, srow, scol, cnt, eg_bf, eu_bf, ed_bf, i, tm=MOE_TILE)
        pending = (alpha, yg, gt2, ln2_g[i][None], ln2_b[i][None])
    xt = _combine(*pending[:1], xt, *pending[1:], tm=512)
    return xt.reshape(B, T, D)
```
